```python
import math
import jax
import jax.numpy as jnp
from jax import lax
import numpy as np

D_MODEL = 2048
BATCH = 4
SEQ = 2048
DEPTH = 4
DEC_BATCH = 128
DEC_SEQ = 1
PAST_LEN = 8192
PAGE_SIZE = 128

N_MIXERS = 4
N_MLA_LAYERS = (DEPTH + N_MIXERS - 1) // N_MIXERS
N_SB_LAYERS = (DEPTH + N_MIXERS - 2) // N_MIXERS
N_GLA_LAYERS = (DEPTH + N_MIXERS - 3) // N_MIXERS
N_CONV_LAYERS = (DEPTH + N_MIXERS - 4) // N_MIXERS
N_DENSE_LAYERS = (DEPTH + 1) // 2
N_MOE_LAYERS = DEPTH // 2
DEEPNORM_ALPHA = (2.0 * DEPTH) ** 0.25
DEEPNORM_BETA = (8.0 * DEPTH) ** -0.25
LN_EPS = 1e-5
RMS_EPS = 1e-6
Q_BLOCK = 128

MLA_HEADS = 16
MLA_NOPE = 128
MLA_ROPE = 64
MLA_V = 128
MLA_Q_RANK = 512
MLA_KV_RANK = 512
MLA_SCALE = (MLA_NOPE + MLA_ROPE) ** -0.5
ROPE_THETA = 10000.0

SB_HEADS = 16
SB_KV_HEADS = 4
SB_HEAD_DIM = 128
SB_GROUP = SB_HEADS // SB_KV_HEADS
SB_SCALE = SB_HEAD_DIM ** -0.5

GLA_HEADS = 4
GLA_QK_DIM = D_MODEL // 2
GLA_V_DIM = D_MODEL
GLA_DK = GLA_QK_DIM // GLA_HEADS
GLA_DV = GLA_V_DIM // GLA_HEADS
GLA_GATE_RANK = 16
GLA_GATE_NORM = 16.0
GLA_CHUNK = 64

CONV_DIM = D_MODEL
CONV_WIDTH = 31

D_FF = 5504
N_EXPERTS = 8
TOP_K = 2
D_FF_EXPERT = 7168

kernel_name = 'hybrid_mla_stickbreak_gla_conformer_step'


def layer_norm(x, g, b):
    xf = x.astype(jnp.float32)
    mu = jnp.mean(xf, axis=-1, keepdims=True)
    var = jnp.mean(jnp.square(xf - mu), axis=-1, keepdims=True)
    return ((xf - mu) * lax.rsqrt(var + LN_EPS) * g.astype(jnp.float32) + b.astype(jnp.float32)).astype(x.dtype)


def rms_norm(x, g):
    xf = x.astype(jnp.float32)
    return (xf * lax.rsqrt(jnp.mean(jnp.square(xf), axis=-1, keepdims=True) + RMS_EPS) * g.astype(jnp.float32)).astype(x.dtype)


def rope(x, pos):
    half = x.shape[-1] // 2
    inv_freq = ROPE_THETA ** (-jnp.arange(half, dtype=jnp.float32) / half)
    ang = pos.astype(jnp.float32)[:, None] * inv_freq[None, :]
    shape = (1, pos.shape[0]) + (1,) * (x.ndim - 3) + (half,)
    cos = jnp.cos(ang).reshape(shape)
    sin = jnp.sin(ang).reshape(shape)
    xf = x.astype(jnp.float32)
    x1, x2 = xf[..., :half], xf[..., half:]
    return jnp.concatenate([x1 * cos - x2 * sin, x2 * cos + x1 * sin], axis=-1).astype(x.dtype)


def gather_pages(pool, layer, page_table):
    g = pool[layer, page_table]
    return g.reshape((g.shape[0], g.shape[1] * g.shape[2]) + g.shape[3:])


def sweep_query_blocks(fn, *qs):
    seq = qs[0].shape[1]
    blk = min(Q_BLOCK, seq)
    n_blocks = seq // blk

    def body(i):
        start = i * blk
        return fn(start, *[lax.dynamic_slice_in_dim(q, start, blk, axis=1) for q in qs])

    out = jnp.moveaxis(lax.map(body, jnp.arange(n_blocks, dtype=jnp.int32)), 0, 1)
    return out.reshape((out.shape[0], seq) + out.shape[3:])


def mla_project(x, pos, w_in, q_norm_g, kv_norm_g, w_q_up, w_kv_up):
    bsz, t, _ = x.shape
    h = x @ w_in
    q_a = h[..., :MLA_Q_RANK]
    c_kv = rms_norm(h[..., MLA_Q_RANK:MLA_Q_RANK + MLA_KV_RANK], kv_norm_g)
    k_pe = rope(h[..., MLA_Q_RANK + MLA_KV_RANK:], pos)
    q = (rms_norm(q_a, q_norm_g) @ w_q_up).reshape(bsz, t, MLA_HEADS, MLA_NOPE + MLA_ROPE)
    q_pe = rope(q[..., MLA_NOPE:], pos)
    q_lat = jnp.einsum('bthn,chn->bthc', q[..., :MLA_NOPE], w_kv_up[..., :MLA_NOPE])
    return q_lat, q_pe, c_kv, k_pe


def mla_attend(q_lat, q_pe, c_segs, r_segs, q_pos, k_pos):
    s = jnp.concatenate([
        jnp.einsum('bthc,blc->bhtl', q_lat, c, preferred_element_type=jnp.float32)
        + jnp.einsum('bthr,blr->bhtl', q_pe, r, preferred_element_type=jnp.float32)
        for c, r in zip(c_segs, r_segs)], axis=-1) * MLA_SCALE
    s = jnp.where(k_pos[None, :] <= q_pos[:, None], s, -jnp.inf)
    p = jax.nn.softmax(s, axis=-1).astype(c_segs[0].dtype)
    out = 0
    start = 0
    for c in c_segs:
        n = c.shape[1]
        out = out + jnp.einsum('bhtl,blc->bthc', p[..., start:start + n], c)
        start += n
    return out


def mla_output(out_lat, w_kv_up, w_o):
    bsz, t = out_lat.shape[:2]
    o = jnp.einsum('bthc,chv->bthv', out_lat, w_kv_up[..., MLA_NOPE:])
    return o.reshape(bsz, t, MLA_HEADS * MLA_V) @ w_o


def mla_prompt(x, pos, w_in, q_norm_g, kv_norm_g, w_q_up, w_kv_up, w_o):
    q_lat, q_pe, c_kv, k_pe = mla_project(x, pos, w_in, q_norm_g, kv_norm_g, w_q_up, w_kv_up)

    def block(start, ql, qp):
        return mla_attend(ql, qp, (c_kv,), (k_pe,), start + jnp.arange(ql.shape[1], dtype=jnp.int32), pos)

    lat = sweep_query_blocks(block, q_lat, q_pe)
    return mla_output(lat, w_kv_up, w_o), c_kv, k_pe


def mla_sample(x, pos, key_pos, c_past, r_past, w_in, q_norm_g, kv_norm_g, w_q_up, w_kv_up, w_o):
    q_lat, q_pe, c_kv, k_pe = mla_project(x, pos, w_in, q_norm_g, kv_norm_g, w_q_up, w_kv_up)
    lat = mla_attend(q_lat, q_pe, (c_past, c_kv), (r_past, k_pe), pos, key_pos)
    return mla_output(lat, w_kv_up, w_o), c_kv, k_pe


def sb_project(x, w_in):
    bsz, t, _ = x.shape
    h = x @ w_in
    nq = SB_HEADS * SB_HEAD_DIM
    nk = SB_KV_HEADS * SB_HEAD_DIM
    q = h[..., :nq].reshape(bsz, t, SB_HEADS, SB_HEAD_DIM)
    k = h[..., nq:nq + nk].reshape(bsz, t, SB_KV_HEADS, SB_HEAD_DIM)
    v = h[..., nq + nk:].reshape(bsz, t, SB_KV_HEADS, SB_HEAD_DIM)
    return q, k, v


def sb_attend(q, k_segs, v_segs, q_pos, k_pos):
    bsz, t = q.shape[:2]
    qg = q.reshape(bsz, t, SB_KV_HEADS, SB_GROUP, SB_HEAD_DIM)
    z = jnp.concatenate([jnp.einsum('btkgd,blkd->bkgtl', qg, k, preferred_element_type=jnp.float32)
                         for k in k_segs], axis=-1) * SB_SCALE
    mask = k_pos[None, :] < q_pos[:, None]
    log_keep = jnp.where(mask, jax.nn.log_sigmoid(-z), 0.0)
    log_rest = lax.cumsum(log_keep, axis=z.ndim - 1, reverse=True) - log_keep
    a = jnp.where(mask, jnp.exp(jax.nn.log_sigmoid(z) + log_rest), 0.0).astype(v_segs[0].dtype)
    out = 0
    start = 0
    for v in v_segs:
        n = v.shape[1]
        out = out + jnp.einsum('bkgtl,blkd->btkgd', a[..., start:start + n], v)
        start += n
    return out.reshape(bsz, t, SB_HEADS * SB_HEAD_DIM)


def sb_prompt(x, pos, w_in, w_o):
    q, k, v = sb_project(x, w_in)

    def block(start, qb):
        return sb_attend(qb, (k,), (v,), start + jnp.arange(qb.shape[1], dtype=jnp.int32), pos)

    return sweep_query_blocks(block, q) @ w_o, k, v


def sb_sample(x, pos, key_pos, k_past, v_past, w_in, w_o):
    q, k, v = sb_project(x, w_in)
    return sb_attend(q, (k_past, k), (v_past, v), pos, key_pos) @ w_o, k, v


def gla_recurrence(q, k, v, log_a, s0):
    bsz, t = q.shape[:2]
    c = math.gcd(t, GLA_CHUNK)
    n = t // c

    def chunks(a):
        return jnp.moveaxis(a.astype(jnp.float32).reshape((bsz, n, c) + a.shape[2:]), 1, 0)

    causal = jnp.tril(jnp.ones((c, c), dtype=bool))

    def step(state, inp):
        qb, kb, vb, ab = inp
        b = jnp.cumsum(ab, axis=1)
        b_last = b[:, -1]
        qe = qb * jnp.exp(b)
        ke = kb * jnp.exp(-b)
        kd = kb * jnp.exp(b_last[:, None] - b)
        att = jnp.where(causal, jnp.einsum('bthd,bshd->bhts', qe, ke), 0.0)
        o = jnp.einsum('bthd,bhdv->bthv', qe, state) + jnp.einsum('bhts,bshv->bthv', att, vb)
        state = jnp.exp(b_last)[..., None] * state + jnp.einsum('bshd,bshv->bhdv', kd, vb)
        return state, o

    s_fin, o = lax.scan(step, s0.astype(jnp.float32), (chunks(q), chunks(k), chunks(v), chunks(log_a)))
    return jnp.moveaxis(o, 0, 1).reshape(bsz, t, GLA_HEADS, GLA_DV), s_fin


def gla_mixer(x, s0, w_in, w_gate_up, b_gate, norm_g, w_o):
    bsz, t, _ = x.shape
    h = x @ w_in
    o1 = GLA_QK_DIM
    o2 = 2 * GLA_QK_DIM
    o3 = o2 + GLA_V_DIM
    o4 = o3 + GLA_V_DIM
    q = h[..., :o1].reshape(bsz, t, GLA_HEADS, GLA_DK) * (GLA_DK ** -0.5)
    k = h[..., o1:o2].reshape(bsz, t, GLA_HEADS, GLA_DK)
    v = h[..., o2:o3].reshape(bsz, t, GLA_HEADS, GLA_DV)
    r = h[..., o3:o4]
    log_a = jax.nn.log_sigmoid((h[..., o4:] @ w_gate_up + b_gate).astype(jnp.float32)) / GLA_GATE_NORM
    o, s_new = gla_recurrence(q, k, v, log_a.reshape(bsz, t, GLA_HEADS, GLA_DK), s0)
    o = rms_norm(o, norm_g).astype(x.dtype).reshape(bsz, t, GLA_V_DIM) * jax.nn.silu(r)
    return o @ w_o, s_new.astype(s0.dtype)


def conv_mixer(x, buf, w_pw1, b_pw1, w_dw, b_dw, ln_g, ln_b, w_pw2, b_pw2):
    a, g = jnp.split(x @ w_pw1 + b_pw1, 2, axis=-1)
    u = a * jax.nn.sigmoid(g)
    ext = jnp.concatenate([buf.astype(u.dtype), u], axis=1)
    y = lax.conv_general_dilated(ext, w_dw[:, None, :].astype(u.dtype), window_strides=(1,), padding='VALID',
                                 dimension_numbers=('NWC', 'WIO', 'NWC'), feature_group_count=CONV_DIM) + b_dw
    y = jax.nn.silu(layer_norm(y, ln_g, ln_b))
    return y @ w_pw2 + b_pw2, ext[:, ext.shape[1] - (CONV_WIDTH - 1):]


def swiglu(x, w_gu, w_down):
    g, u = jnp.split(x @ w_gu, 2, axis=-1)
    return (jax.nn.silu(g) * u) @ w_down


def moe_ffn(x, w_router, w_gu, w_down, layer):
    logits = (x @ w_router[layer]).astype(jnp.float32)
    top_val, top_idx = lax.top_k(logits, TOP_K)
    top_w = jax.nn.softmax(top_val, axis=-1)
    gates = jnp.sum(jax.nn.one_hot(top_idx, N_EXPERTS, dtype=jnp.float32) * top_w[..., None], axis=-2).astype(x.dtype)
    y = jnp.zeros_like(x)
    for e in range(N_EXPERTS):
        y = y + gates[..., e:e + 1] * swiglu(x, w_gu[layer, e], w_down[layer, e])
    return y


def setup_inputs(seed: int = 0) -> dict:
    key = jax.random.key(seed)
    keys = iter(jax.random.split(key, 64))

    def normal(shape, scale=1.0):
        return jax.random.normal(next(keys), shape, dtype=jnp.float32) * scale

    def gain(shape):
        return 1.0 + normal(shape, 0.02)

    n_pages = PAST_LEN // PAGE_SIZE
    n_used = DEC_BATCH * n_pages
    n_pool = n_used + max(n_used // 4, 1)
    page_table = jax.random.permutation(next(keys), n_pool)[:n_used].reshape(DEC_BATCH, n_pages).astype(jnp.int32)
    d = D_MODEL
    beta = DEEPNORM_BETA
    return {
        'x_prompt': normal((BATCH, SEQ, d)),
        'x_sample': normal((DEC_BATCH, DEC_SEQ, d)),
        'cache_mla_latent': normal((N_MLA_LAYERS, n_pool, PAGE_SIZE, MLA_KV_RANK)),
        'cache_mla_rope': normal((N_MLA_LAYERS, n_pool, PAGE_SIZE, MLA_ROPE)),
        'cache_sb_k': normal((N_SB_LAYERS, n_pool, PAGE_SIZE, SB_KV_HEADS, SB_HEAD_DIM)),
        'cache_sb_v': normal((N_SB_LAYERS, n_pool, PAGE_SIZE, SB_KV_HEADS, SB_HEAD_DIM)),
        'state_gla': normal((N_GLA_LAYERS, DEC_BATCH, GLA_HEADS, GLA_DK, GLA_DV), 0.5),
        'state_conv': normal((N_CONV_LAYERS, DEC_BATCH, CONV_WIDTH - 1, CONV_DIM), 0.5),
        'page_table': page_table,
        'ln_g': gain((DEPTH, 2, d)),
        'ln_b': normal((DEPTH, 2, d), 0.02),
        'mla_w_in': normal((N_MLA_LAYERS, d, MLA_Q_RANK + MLA_KV_RANK + MLA_ROPE), d ** -0.5),
        'mla_q_norm': gain((N_MLA_LAYERS, MLA_Q_RANK)),
        'mla_kv_norm': gain((N_MLA_LAYERS, MLA_KV_RANK)),
        'mla_w_q_up': normal((N_MLA_LAYERS, MLA_Q_RANK, MLA_HEADS * (MLA_NOPE + MLA_ROPE)), MLA_Q_RANK ** -0.5),
        'mla_w_kv_up': normal((N_MLA_LAYERS, MLA_KV_RANK, MLA_HEADS, MLA_NOPE + MLA_V), MLA_KV_RANK ** -0.5),
        'mla_w_o': normal((N_MLA_LAYERS, MLA_HEADS * MLA_V, d), beta * (MLA_HEADS * MLA_V) ** -0.5),
        'sb_w_in': normal((N_SB_LAYERS, d, (SB_HEADS + 2 * SB_KV_HEADS) * SB_HEAD_DIM), d ** -0.5),
        'sb_w_o': normal((N_SB_LAYERS, SB_HEADS * SB_HEAD_DIM, d), beta * (SB_HEADS * SB_HEAD_DIM) ** -0.5),
        'gla_w_in': normal((N_GLA_LAYERS, d, 2 * GLA_QK_DIM + 2 * GLA_V_DIM + GLA_GATE_RANK), d ** -0.5),
        'gla_w_gate_up': normal((N_GLA_LAYERS, GLA_GATE_RANK, GLA_QK_DIM), GLA_GATE_RANK ** -0.5),
        'gla_b_gate': normal((N_GLA_LAYERS, GLA_QK_DIM), 0.02),
        'gla_norm': gain((N_GLA_LAYERS, GLA_DV)),
        'gla_w_o': normal((N_GLA_LAYERS, GLA_V_DIM, d), beta * GLA_V_DIM ** -0.5),
        'conv_w_pw1': normal((N_CONV_LAYERS, d, 2 * CONV_DIM), d ** -0.5),
        'conv_b_pw1': normal((N_CONV_LAYERS, 2 * CONV_DIM), 0.02),
        'conv_w_dw': normal((N_CONV_LAYERS, CONV_WIDTH, CONV_DIM), CONV_WIDTH ** -0.5),
        'conv_b_dw': normal((N_CONV_LAYERS, CONV_DIM), 0.02),
        'conv_ln_g': gain((N_CONV_LAYERS, CONV_DIM)),
        'conv_ln_b': normal((N_CONV_LAYERS, CONV_DIM), 0.02),
        'conv_w_pw2': normal((N_CONV_LAYERS, CONV_DIM, d), beta * CONV_DIM ** -0.5),
        'conv_b_pw2': normal((N_CONV_LAYERS, d), 0.02),
        'ffn_w_gu': normal((N_DENSE_LAYERS, d, 2 * D_FF), d ** -0.5),
        'ffn_w_down': normal((N_DENSE_LAYERS, D_FF, d), beta * D_FF ** -0.5),
        'moe_w_router': normal((N_MOE_LAYERS, d, N_EXPERTS), d ** -0.5),
        'moe_w_gu': normal((N_MOE_LAYERS, N_EXPERTS, d, 2 * D_FF_EXPERT), d ** -0.5),
        'moe_w_down': normal((N_MOE_LAYERS, N_EXPERTS, D_FF_EXPERT, d), beta * D_FF_EXPERT ** -0.5),
    }


def reference(x_prompt, x_sample, cache_mla_latent, cache_mla_rope, cache_sb_k, cache_sb_v, state_gla, state_conv,
              page_table, ln_g, ln_b, mla_w_in, mla_q_norm, mla_kv_norm, mla_w_q_up, mla_w_kv_up, mla_w_o,
              sb_w_in, sb_w_o, gla_w_in, gla_w_gate_up, gla_b_gate, gla_norm, gla_w_o,
              conv_w_pw1, conv_b_pw1, conv_w_dw, conv_b_dw, conv_ln_g, conv_ln_b, conv_w_pw2, conv_b_pw2,
              ffn_w_gu, ffn_w_down, moe_w_router, moe_w_gu, moe_w_down):
    bsz, seq, _ = x_prompt.shape
    dec_seq = x_sample.shape[1]
    past = page_table.shape[1] * cache_mla_latent.shape[2]
    pos_p = jnp.arange(seq, dtype=jnp.int32)
    pos_s = past + jnp.arange(dec_seq, dtype=jnp.int32)
    key_pos_s = jnp.arange(past + dec_seq, dtype=jnp.int32)
    xp, xs = x_prompt, x_sample
    mla_lat_p, mla_rope_p, mla_lat_s, mla_rope_s = [], [], [], []
    sb_k_p, sb_v_p, sb_k_s, sb_v_s = [], [], [], []
    gla_p, gla_s, conv_p, conv_s = [], [], [], []
    for i in range(DEPTH):
        j = i // N_MIXERS
        mixer = i % N_MIXERS
        if mixer == 0:
            w = (mla_w_in[j], mla_q_norm[j], mla_kv_norm[j], mla_w_q_up[j], mla_w_kv_up[j], mla_w_o[j])
            yp, c_p, r_p = mla_prompt(xp, pos_p, *w)
            ys, c_s, r_s = mla_sample(xs, pos_s, key_pos_s, gather_pages(cache_mla_latent, j, page_table),
                                      gather_pages(cache_mla_rope, j, page_table), *w)
            mla_lat_p.append(c_p)
            mla_rope_p.append(r_p)
            mla_lat_s.append(c_s)
            mla_rope_s.append(r_s)
        elif mixer == 1:
            w = (sb_w_in[j], sb_w_o[j])
            yp, k_p, v_p = sb_prompt(xp, pos_p, *w)
            ys, k_s, v_s = sb_sample(xs, pos_s, key_pos_s, gather_pages(cache_sb_k, j, page_table),
                                     gather_pages(cache_sb_v, j, page_table), *w)
            sb_k_p.append(k_p)
            sb_v_p.append(v_p)
            sb_k_s.append(k_s)
            sb_v_s.append(v_s)
        elif mixer == 2:
            w = (gla_w_in[j], gla_w_gate_up[j], gla_b_gate[j], gla_norm[j], gla_w_o[j])
            yp, st_p = gla_mixer(xp, jnp.zeros((bsz, GLA_HEADS, GLA_DK, GLA_DV), state_gla.dtype), *w)
            ys, st_s = gla_mixer(xs, state_gla[j], *w)
            gla_p.append(st_p)
            gla_s.append(st_s)
        else:
            w = (conv_w_pw1[j], conv_b_pw1[j], conv_w_dw[j], conv_b_dw[j], conv_ln_g[j], conv_ln_b[j],
                 conv_w_pw2[j], conv_b_pw2[j])
            yp, buf_p = conv_mixer(xp, jnp.zeros((bsz, CONV_WIDTH - 1, CONV_DIM), state_conv.dtype), *w)
            ys, buf_s = conv_mixer(xs, state_conv[j], *w)
            conv_p.append(buf_p)
            conv_s.append(buf_s)
        xp = layer_norm(DEEPNORM_ALPHA * xp + yp, ln_g[i, 0], ln_b[i, 0])
        xs = layer_norm(DEEPNORM_ALPHA * xs + ys, ln_g[i, 0], ln_b[i, 0])
        f = i // 2
        if i % 2 == 0:
            fp = swiglu(xp, ffn_w_gu[f], ffn_w_down[f])
            fs = swiglu(xs, ffn_w_gu[f], ffn_w_down[f])
        else:
            fp = moe_ffn(xp, moe_w_router, moe_w_gu, moe_w_down, f)
            fs = moe_ffn(xs, moe_w_router, moe_w_gu, moe_w_down, f)
        xp = layer_norm(DEEPNORM_ALPHA * xp + fp, ln_g[i, 1], ln_b[i, 1])
        xs = layer_norm(DEEPNORM_ALPHA * xs + fs, ln_g[i, 1], ln_b[i, 1])
    return (xp, xs,
            jnp.stack(mla_lat_p), jnp.stack(mla_rope_p), jnp.stack(mla_lat_s), jnp.stack(mla_rope_s),
            jnp.stack(sb_k_p), jnp.stack(sb_v_p), jnp.stack(sb_k_s), jnp.stack(sb_v_s),
            jnp.stack(gla_p), jnp.stack(gla_s), jnp.stack(conv_p), jnp.stack(conv_s))
```

```python
import functools

import jax
import jax.numpy as jnp
from jax import lax
from jax.experimental import pallas as pl
from jax.experimental.pallas import tpu as pltpu

F32 = jnp.float32
BF16 = jnp.bfloat16

DEPTH = 4
DEEPNORM_ALPHA = (2.0 * DEPTH) ** 0.25
LN_EPS = 1e-5
RMS_EPS = 1e-6
MLA_HEADS = 16
MLA_NOPE = 128
MLA_ROPE = 64
MLA_V = 128
MLA_SCALE = (MLA_NOPE + MLA_ROPE) ** -0.5
ROPE_THETA = 10000.0
SB_HEADS = 16
SB_KV_HEADS = 4
SB_HEAD_DIM = 128
SB_GROUP = SB_HEADS // SB_KV_HEADS
SB_SCALE = SB_HEAD_DIM ** -0.5
GLA_HEADS = 4
GLA_GATE_NORM = 16.0
GLA_CHUNK = 64
CONV_WIDTH = 31
TOP_K = 2

V7X_VMEM_LIMIT = 56 * 1024 * 1024
EXP_ZERO_BELOW = -104.0
LANE = 128


def _cp(*sem):
    return pltpu.CompilerParams(dimension_semantics=sem, vmem_limit_bytes=V7X_VMEM_LIMIT)


def _tile(n, target, mult):
    best = None
    for t in range(mult, min(n, target) + 1, mult):
        if n % t == 0:
            best = t
    return best if best is not None else n


def _sigmoid(x):
    return 1.0 / (1.0 + jnp.exp(-x))


def _softplus(x):
    return jnp.maximum(x, 0.0) + jnp.log(1.0 + jnp.exp(-jnp.abs(x)))


def _split_bf16(x):
    hi = x.astype(BF16)
    lo = (x - hi.astype(F32)).astype(BF16)
    return hi, lo


def _dot(a, b):
    return jnp.dot(a, b, preferred_element_type=F32)


def _dot_nt(a, b):
    return lax.dot_general(a, b, (((1,), (1,)), ((), ())), preferred_element_type=F32)


def _dot_tn(a, b):
    return lax.dot_general(a, b, (((0,), (0,)), ((), ())), preferred_element_type=F32)


def _mm_body(*refs, has_bias):
    if has_bias:
        x_ref, w_ref, b_ref, o_ref = refs
    else:
        x_ref, w_ref, o_ref = refs
    acc = _dot(x_ref[...].astype(BF16), w_ref[...].astype(BF16))
    if has_bias:
        acc = acc + b_ref[...]
    o_ref[...] = acc.astype(o_ref.dtype)


def mm(x, w, bias=None, out_dtype=F32, tm_target=1040, tn_target=512):
    m, k = x.shape
    n = w.shape[1]
    tm = _tile(m, tm_target, 16)
    tn = n if n <= tn_target else tn_target
    grid = (m // tm, pl.cdiv(n, tn))
    in_specs = [pl.BlockSpec((tm, k), lambda i, j: (i, 0)), pl.BlockSpec((k, tn), lambda i, j: (0, j))]
    args = [x, w]
    if bias is not None:
        in_specs.append(pl.BlockSpec((1, tn), lambda i, j: (0, j)))
        args.append(bias.reshape(1, n))
    return pl.pallas_call(
        functools.partial(_mm_body, has_bias=bias is not None),
        grid=grid, in_specs=in_specs, out_specs=pl.BlockSpec((tm, tn), lambda i, j: (i, j)),
        out_shape=jax.ShapeDtypeStruct((m, n), out_dtype),
        compiler_params=_cp("parallel", "parallel"), name="mm")(*args)


def _mm_acc_body(x_ref, w_ref, o_ref):
    @pl.when(pl.program_id(2) == 0)
    def _():
        o_ref[...] = jnp.zeros_like(o_ref)

    o_ref[...] += _dot(x_ref[...], w_ref[...].astype(BF16))


def mm_acc(x, w, x_map, w_map, x_block, w_block, n_k, m, n, tm, tn):
    return pl.pallas_call(
        _mm_acc_body, grid=(m // tm, n // tn, n_k),
        in_specs=[pl.BlockSpec(x_block, x_map), pl.BlockSpec(w_block, w_map)],
        out_specs=pl.BlockSpec((tm, tn), lambda i, j, k: (i, j)),
        out_shape=jax.ShapeDtypeStruct((m, n), F32),
        compiler_params=_cp("parallel", "parallel", "arbitrary"), name="mm_acc")(x, w)


def _glu_body(*refs, mode, has_bias, has_gate):
    refs = list(refs)
    x_ref, wa_ref, wb_ref = refs[:3]
    o_ref = refs[-1]
    pos = 3
    x = x_ref[...]
    a = _dot(x, wa_ref[...].astype(BF16))
    b = _dot(x, wb_ref[...].astype(BF16))
    if has_bias:
        a = a + refs[pos][...]
        b = b + refs[pos + 1][...]
        pos += 2
    if mode == "swiglu":
        h = a * _sigmoid(a) * b
    else:
        h = a * _sigmoid(b)
    if has_gate:
        g = refs[pos][...]
        e = pl.program_id(0)
        sel = lax.broadcasted_iota(jnp.int32, g.shape, 1) == e
        h = h * jnp.sum(jnp.where(sel, g, 0.0), axis=1, keepdims=True)
    o_ref[...] = h.astype(o_ref.dtype)


def _res_ln_body(*refs, has_bias):
    if has_bias:
        x_ref, y_ref, bias_ref, g_ref, b_ref, o32_ref, o16_ref = refs
    else:
        x_ref, y_ref, g_ref, b_ref, o32_ref, o16_ref = refs
    z = DEEPNORM_ALPHA * x_ref[...] + y_ref[...]
    if has_bias:
        z = z + bias_ref[...]
    mu = jnp.mean(z, axis=-1, keepdims=True)
    zc = z - mu
    var = jnp.mean(zc * zc, axis=-1, keepdims=True)
    out = zc * lax.rsqrt(var + LN_EPS) * g_ref[...] + b_ref[...]
    o32_ref[...] = out
    o16_ref[...] = out.astype(BF16)


def res_ln(x, y, g, b, bias=None):
    m, d = x.shape
    tm = _tile(m, 416, 16)
    row = pl.BlockSpec((tm, d), lambda i: (i, 0))
    vec = pl.BlockSpec((1, d), lambda i: (0, 0))
    args = [x, y] + ([bias.reshape(1, d)] if bias is not None else []) + [g.reshape(1, d), b.reshape(1, d)]
    in_specs = [row, row] + ([vec] if bias is not None else []) + [vec, vec]
    return pl.pallas_call(
        functools.partial(_res_ln_body, has_bias=bias is not None),
        grid=(m // tm,), in_specs=in_specs, out_specs=[row, row],
        out_shape=[jax.ShapeDtypeStruct((m, d), F32), jax.ShapeDtypeStruct((m, d), BF16)],
        compiler_params=_cp("parallel"), name="res_ln")(*args)


def _rope_tables(pos, half):
    inv_freq = ROPE_THETA ** (-jnp.arange(half, dtype=F32) / half)
    ang = pos.astype(F32)[:, None] * inv_freq[None, :]
    return jnp.cos(ang), jnp.sin(ang)


def _mla_a_body(h_ref, cos_ref, sin_ref, qg_ref, kvg_ref, c32_ref, c16_ref, r32_ref, r16_ref, qn_ref, *, qr, cr):
    h = h_ref[...]
    qa = h[:, :qr]
    ckv = h[:, qr:qr + cr]
    kr = h[:, qr + cr:]
    qn = qa * lax.rsqrt(jnp.mean(qa * qa, axis=-1, keepdims=True) + RMS_EPS) * qg_ref[...]
    c = ckv * lax.rsqrt(jnp.mean(ckv * ckv, axis=-1, keepdims=True) + RMS_EPS) * kvg_ref[...]
    half = kr.shape[1] // 2
    x1, x2 = kr[:, :half], kr[:, half:]
    cos, sin = cos_ref[...], sin_ref[...]
    r = jnp.concatenate([x1 * cos - x2 * sin, x2 * cos + x1 * sin], axis=-1)
    qn_ref[...] = qn.astype(BF16)
    c32_ref[...] = c
    c16_ref[...] = c.astype(BF16)
    r32_ref[...] = r
    r16_ref[...] = r.astype(BF16)


def _mla_c_body(q_ref, cos_ref, sin_ref, wuk_ref, qlat_ref, qpe_ref, *, heads, nope, rope):
    cos, sin = cos_ref[...], sin_ref[...]
    half = rope // 2
    for h in range(heads):
        qn = q_ref[:, h * nope:(h + 1) * nope].astype(BF16)
        qlat_ref[h] = (_dot(qn, wuk_ref[h]) * MLA_SCALE).astype(BF16)
        base = heads * nope + h * rope
        blk = q_ref[:, base // LANE * LANE:base // LANE * LANE + LANE]
        off = base % LANE
        x1 = blk[:, off:off + half]
        x2 = blk[:, off + half:off + rope]
        pe = jnp.concatenate([x1 * cos - x2 * sin, x2 * cos + x1 * sin], axis=-1)
        qpe_ref[h] = (pe * MLA_SCALE).astype(BF16)


def _mla_attn_prompt_body(qlat_ref, qpe_ref, c_ref, r_ref, wuv_ref, o_ref, m_ref, l_ref, acc_ref, *, heads, tq):
    qi = pl.program_id(1)
    kj = pl.program_id(2)
    nk = pl.num_programs(2)
    rows = heads * tq

    @pl.when(kj == 0)
    def _():
        m_ref[...] = jnp.full_like(m_ref, -jnp.inf)
        l_ref[...] = jnp.zeros_like(l_ref)
        acc_ref[...] = jnp.zeros_like(acc_ref)

    def step(diag):
        q = qlat_ref[...].reshape(rows, qlat_ref.shape[-1])
        qp = qpe_ref[...].reshape(rows, qpe_ref.shape[-1])
        c = c_ref[...]
        s = _dot_nt(q, c) + _dot_nt(qp, r_ref[...])
        tk = s.shape[-1]
        if diag:
            s3 = s.reshape(heads, tq, tk)
            qpos = lax.broadcasted_iota(jnp.int32, (1, tq, tk), 1)
            kpos = lax.broadcasted_iota(jnp.int32, (1, tq, tk), 2)
            s = jnp.where(kpos <= qpos, s3, -jnp.inf).reshape(rows, tk)
        m_old = m_ref[...]
        m_new = jnp.maximum(m_old, jnp.max(s, axis=-1, keepdims=True))
        p = jnp.exp(s - m_new)
        alpha = jnp.exp(m_old - m_new)
        l_ref[...] = alpha * l_ref[...] + jnp.sum(p, axis=-1, keepdims=True)
        acc_ref[...] = alpha * acc_ref[...] + _dot(p.astype(BF16), c)
        m_ref[...] = m_new

    @pl.when(kj < qi)
    def _():
        step(False)

    @pl.when(kj == qi)
    def _():
        step(True)

    @pl.when(kj == nk - 1)
    def _():
        v = MLA_V
        for h in range(heads):
            lat = acc_ref[h * tq:(h + 1) * tq, :] / l_ref[h * tq:(h + 1) * tq, :]
            o_ref[:, h * v:(h + 1) * v] = _dot(lat.astype(BF16), wuv_ref[h]).astype(o_ref.dtype)


def _mla_attn_sample_body(pt_ref, *refs, heads, n_group):
    del pt_ref
    qlat_ref, qpe_ref, cnew_ref, rnew_ref = refs[:4]
    c_refs = refs[4:4 + n_group]
    r_refs = refs[4 + n_group:4 + 2 * n_group]
    o_ref = refs[4 + 2 * n_group]
    m_ref, l_ref, acc_ref = refs[4 + 2 * n_group + 1:]
    j = pl.program_id(1)
    nj = pl.num_programs(1)

    @pl.when(j == 0)
    def _():
        m_ref[...] = jnp.full_like(m_ref, -jnp.inf)
        l_ref[...] = jnp.zeros_like(l_ref)
        acc_ref[...] = jnp.zeros_like(acc_ref)

    q = qlat_ref[...]
    qp = qpe_ref[...]
    for g in range(n_group):
        c = c_refs[g][...].astype(BF16)
        r = r_refs[g][...].astype(BF16)
        s = _dot_nt(q, c) + _dot_nt(qp, r)
        m_old = m_ref[...]
        m_new = jnp.maximum(m_old, jnp.max(s, axis=-1, keepdims=True))
        p = jnp.exp(s - m_new)
        alpha = jnp.exp(m_old - m_new)
        l_ref[...] = alpha * l_ref[...] + jnp.sum(p, axis=-1, keepdims=True)
        acc_ref[...] = alpha * acc_ref[...] + _dot(p.astype(BF16), c)
        m_ref[...] = m_new

    @pl.when(j == nj - 1)
    def _():
        cn = cnew_ref[...].astype(F32)
        s = (jnp.sum(q.astype(F32) * cn, axis=-1, keepdims=True)
             + jnp.sum(qp.astype(F32) * rnew_ref[...].astype(F32), axis=-1, keepdims=True))
        m_old = m_ref[...]
        m_new = jnp.maximum(m_old, s)
        p = jnp.exp(s - m_new)
        alpha = jnp.exp(m_old - m_new)
        l = alpha * l_ref[...] + p
        acc = alpha * acc_ref[...] + p.astype(BF16).astype(F32) * cn
        o_ref[...] = (acc / l).astype(o_ref.dtype)


def _mla_out_sample_body(x_ref, wuv_ref, o_ref, *, heads):
    v = MLA_V
    for h in range(heads):
        o_ref[:, h * v:(h + 1) * v] = _dot(x_ref[h], wuv_ref[h]).astype(o_ref.dtype)


def mla_mixer(xb, n_prompt_seqs, seq, past, cache_lat, cache_rope, page_table,
              w_in, q_norm, kv_norm, w_q_up, w_kv_up, w_o):
    m = xb.shape[0]
    mp = n_prompt_seqs * seq
    ms = m - mp
    qr, cr = q_norm.shape[0], kv_norm.shape[0]
    heads, nope, rope, vdim = MLA_HEADS, MLA_NOPE, MLA_ROPE, MLA_V
    half = rope // 2

    pos = jnp.concatenate([jnp.tile(jnp.arange(seq, dtype=jnp.int32), n_prompt_seqs),
                           jnp.full((ms,), past, dtype=jnp.int32)])
    cos, sin = _rope_tables(pos, half)

    h = mm(xb, w_in)
    tm = _tile(m, 416, 16)
    row = lambda w: pl.BlockSpec((tm, w), lambda i: (i, 0))
    vec = lambda w: pl.BlockSpec((1, w), lambda i: (0, 0))
    c32, c16, r32, r16, qn = pl.pallas_call(
        functools.partial(_mla_a_body, qr=qr, cr=cr), grid=(m // tm,),
        in_specs=[row(qr + cr + rope), row(half), row(half), vec(qr), vec(cr)],
        out_specs=[row(cr), row(cr), row(rope), row(rope), row(qr)],
        out_shape=[jax.ShapeDtypeStruct((m, cr), F32), jax.ShapeDtypeStruct((m, cr), BF16),
                   jax.ShapeDtypeStruct((m, rope), F32), jax.ShapeDtypeStruct((m, rope), BF16),
                   jax.ShapeDtypeStruct((m, qr), BF16)],
        compiler_params=_cp("parallel"), name="mla_a")(h, cos, sin, q_norm.reshape(1, qr), kv_norm.reshape(1, cr))

    wq = w_q_up.reshape(qr, heads, nope + rope)
    wq = jnp.concatenate([wq[:, :, :nope].reshape(qr, heads * nope), wq[:, :, nope:].reshape(qr, heads * rope)], axis=1)
    q = mm(qn, wq)
    wuk = jnp.transpose(w_kv_up[:, :, :nope], (1, 2, 0)).astype(BF16)
    wuv = jnp.transpose(w_kv_up[:, :, nope:], (1, 0, 2)).astype(BF16)
    tm2 = _tile(m, 208, 16)
    qlat, qpe = pl.pallas_call(
        functools.partial(_mla_c_body, heads=heads, nope=nope, rope=rope), grid=(m // tm2,),
        in_specs=[pl.BlockSpec((tm2, heads * (nope + rope)), lambda i: (i, 0)),
                  pl.BlockSpec((tm2, half), lambda i: (i, 0)), pl.BlockSpec((tm2, half), lambda i: (i, 0)),
                  pl.BlockSpec((heads, nope, cr), lambda i: (0, 0, 0))],
        out_specs=[pl.BlockSpec((heads, tm2, cr), lambda i: (0, i, 0)),
                   pl.BlockSpec((heads, tm2, rope), lambda i: (0, i, 0))],
        out_shape=[jax.ShapeDtypeStruct((heads, m, cr), BF16), jax.ShapeDtypeStruct((heads, m, rope), BF16)],
        compiler_params=_cp("parallel"), name="mla_c")(q, cos, sin, wuk)

    tq = _tile(seq, 256, 16)
    nq = seq // tq
    o_p = pl.pallas_call(
        functools.partial(_mla_attn_prompt_body, heads=heads, tq=tq),
        grid=(n_prompt_seqs, nq, nq),
        in_specs=[pl.BlockSpec((heads, tq, cr), lambda b, i, j: (0, b * nq + i, 0)),
                  pl.BlockSpec((heads, tq, rope), lambda b, i, j: (0, b * nq + i, 0)),
                  pl.BlockSpec((tq, cr), lambda b, i, j: (b * nq + jnp.minimum(i, j), 0)),
                  pl.BlockSpec((tq, rope), lambda b, i, j: (b * nq + jnp.minimum(i, j), 0)),
                  pl.BlockSpec((heads, cr, vdim), lambda b, i, j: (0, 0, 0))],
        out_specs=pl.BlockSpec((tq, heads * vdim), lambda b, i, j: (b * nq + i, 0)),
        out_shape=jax.ShapeDtypeStruct((mp, heads * vdim), BF16),
        scratch_shapes=[pltpu.VMEM((heads * tq, 1), F32), pltpu.VMEM((heads * tq, 1), F32),
                        pltpu.VMEM((heads * tq, cr), F32)],
        compiler_params=_cp("parallel", "parallel", "arbitrary"), name="mla_attn_prompt")(qlat, qpe, c16, r16, wuv)

    n_pages = page_table.shape[1]
    page = cache_lat.shape[2]
    n_group = 8 if n_pages % 8 == 0 else 1
    qlat_s = jnp.transpose(qlat[:, mp:], (1, 0, 2))
    qpe_s = jnp.transpose(qpe[:, mp:], (1, 0, 2))
    cnew = c16[mp:].reshape(ms, 1, cr)
    rnew = r16[mp:].reshape(ms, 1, rope)

    def page_map(g):
        return lambda b, j, pt: (0, pt[b * n_pages + j * n_group + g], 0, 0)

    seq_map = lambda b, j, pt: (b, 0, 0)
    lat_s = pl.pallas_call(
        functools.partial(_mla_attn_sample_body, heads=heads, n_group=n_group),
        grid_spec=pltpu.PrefetchScalarGridSpec(
            num_scalar_prefetch=1, grid=(ms, n_pages // n_group),
            in_specs=[pl.BlockSpec((None, heads, cr), seq_map), pl.BlockSpec((None, heads, rope), seq_map),
                      pl.BlockSpec((None, 1, cr), seq_map), pl.BlockSpec((None, 1, rope), seq_map)]
                     + [pl.BlockSpec((None, None, page, cr), page_map(g)) for g in range(n_group)]
                     + [pl.BlockSpec((None, None, page, rope), page_map(g)) for g in range(n_group)],
            out_specs=pl.BlockSpec((None, heads, cr), seq_map),
            scratch_shapes=[pltpu.VMEM((heads, 1), F32), pltpu.VMEM((heads, 1), F32), pltpu.VMEM((heads, cr), F32)]),
        out_shape=jax.ShapeDtypeStruct((ms, heads, cr), BF16),
        compiler_params=_cp("parallel", "arbitrary"), name="mla_attn_sample")(
            page_table.reshape(-1), qlat_s, qpe_s, cnew, rnew,
            *([cache_lat] * n_group), *([cache_rope] * n_group))
    o_s = pl.pallas_call(
        functools.partial(_mla_out_sample_body, heads=heads), grid=(1,),
        in_specs=[pl.BlockSpec((heads, ms, cr), lambda i: (0, 0, 0)), pl.BlockSpec((heads, cr, vdim), lambda i: (0, 0, 0))],
        out_specs=pl.BlockSpec((ms, heads * vdim), lambda i: (0, 0)),
        out_shape=jax.ShapeDtypeStruct((ms, heads * vdim), BF16),
        compiler_params=_cp("arbitrary"), name="mla_out_sample")(jnp.transpose(lat_s, (1, 0, 2)), wuv)

    y = mm(jnp.concatenate([o_p, o_s], axis=0), w_o)
    return y, c32, r32


def _sb_weights(z, carry, u, mask):
    sp = _softplus(z)
    lk = -sp if mask is None else jnp.where(mask, -sp, 0.0)
    hi, lo = _split_bf16(lk)
    rest = _dot(hi, u) + _dot(lo, u)
    a = jnp.exp(z - sp + rest + carry)
    if mask is not None:
        a = jnp.where(mask, a, 0.0)
    return a, carry + jnp.sum(lk, axis=-1, keepdims=True)


def _later_key_matrix(n):
    j = lax.broadcasted_iota(jnp.int32, (n, n), 0)
    s = lax.broadcasted_iota(jnp.int32, (n, n), 1)
    return jnp.where(j > s, 1.0, 0.0).astype(BF16)


def _sb_attn_prompt_body(q_ref, k_ref, v_ref, o_ref, carry_ref, acc_ref, *, tq):
    qi = pl.program_id(2)
    kj = pl.program_id(3)
    nk = pl.num_programs(3)
    d = SB_HEAD_DIM
    grp = SB_GROUP

    @pl.when(kj == 0)
    def _():
        carry_ref[...] = jnp.zeros_like(carry_ref)
        acc_ref[...] = jnp.zeros_like(acc_ref)

    def step(diag):
        qb = q_ref[...]
        q4 = jnp.concatenate([qb[:, g * d:(g + 1) * d] for g in range(grp)], axis=0).astype(BF16)
        k = k_ref[...].astype(BF16)
        z = _dot_nt(q4, k) * SB_SCALE
        tk = z.shape[-1]
        mask = None
        if diag:
            r = lax.broadcasted_iota(jnp.int32, (grp, tq, tk), 1).reshape(grp * tq, tk)
            c = lax.broadcasted_iota(jnp.int32, (grp * tq, tk), 1)
            mask = c < r
        a, carry = _sb_weights(z, carry_ref[...], _later_key_matrix(tk), mask)
        acc_ref[...] += _dot(a.astype(BF16), v_ref[...].astype(BF16))
        carry_ref[...] = carry

    @pl.when(kj == 0)
    def _():
        step(True)

    @pl.when(jnp.logical_and(kj > 0, kj <= qi))
    def _():
        @pl.when(jnp.max(carry_ref[...]) > EXP_ZERO_BELOW)
        def _():
            step(False)

    @pl.when(kj == nk - 1)
    def _():
        for g in range(grp):
            o_ref[:, g * d:(g + 1) * d] = acc_ref[g * tq:(g + 1) * tq, :].astype(o_ref.dtype)


def _sb_attn_sample_body(pt_ref, q_ref, cin_ref, ain_ref, k_ref, v_ref, cout_ref, aout_ref, *, check):
    del pt_ref
    j = pl.program_id(1)
    grp = SB_GROUP

    @pl.when(j == 0)
    def _():
        cout_ref[...] = cin_ref[...]
        aout_ref[...] = ain_ref[...]

    def step():
        q = q_ref[...].astype(BF16)
        page = k_ref.shape[0]
        z = jnp.concatenate(
            [_dot_nt(q[h * grp:(h + 1) * grp], k_ref[:, h, :].astype(BF16)) for h in range(SB_KV_HEADS)], axis=0) * SB_SCALE
        a, carry = _sb_weights(z, cout_ref[...], _later_key_matrix(page), None)
        ab = a.astype(BF16)
        for h in range(SB_KV_HEADS):
            aout_ref[h * grp:(h + 1) * grp, :] += _dot(ab[h * grp:(h + 1) * grp], v_ref[:, h, :].astype(BF16))
        cout_ref[...] = carry

    if check:
        @pl.when(jnp.max(cout_ref[...]) > EXP_ZERO_BELOW)
        def _():
            step()
    else:
        step()


def _sb_sample_pages(q_s, carry, acc, cache_k, cache_v, page_table, first, count, check):
    ms, heads, d = q_s.shape
    n_pages = page_table.shape[1]
    page = cache_k.shape[2]
    kvh = cache_k.shape[3]
    seq_map = lambda b, j, pt: (b, 0, 0)
    page_map = lambda b, j, pt: (0, pt[b * n_pages + first - j], 0, 0, 0)
    return pl.pallas_call(
        functools.partial(_sb_attn_sample_body, check=check),
        grid_spec=pltpu.PrefetchScalarGridSpec(
            num_scalar_prefetch=1, grid=(ms, count),
            in_specs=[pl.BlockSpec((None, heads, d), seq_map), pl.BlockSpec((None, heads, 1), seq_map),
                      pl.BlockSpec((None, heads, d), seq_map),
                      pl.BlockSpec((None, None, page, kvh, d), page_map),
                      pl.BlockSpec((None, None, page, kvh, d), page_map)],
            out_specs=[pl.BlockSpec((None, heads, 1), seq_map), pl.BlockSpec((None, heads, d), seq_map)]),
        out_shape=[jax.ShapeDtypeStruct((ms, heads, 1), F32), jax.ShapeDtypeStruct((ms, heads, d), F32)],
        compiler_params=_cp("parallel", "arbitrary"), name="sb_attn_sample")(
            page_table.reshape(-1), q_s, carry, acc, cache_k, cache_v)


def sb_mixer(xb, n_prompt_seqs, seq, cache_k, cache_v, page_table, w_in, w_o):
    m = xb.shape[0]
    mp = n_prompt_seqs * seq
    ms = m - mp
    heads, kvh, d, grp = SB_HEADS, SB_KV_HEADS, SB_HEAD_DIM, SB_GROUP
    nq_cols = heads * d
    nk_cols = kvh * d
    h = mm(xb, w_in)
    k_new = h[:, nq_cols:nq_cols + nk_cols]
    v_new = h[:, nq_cols + nk_cols:]

    tq = _tile(seq, 256, 8)
    nq = seq // tq
    kcol = nq_cols // d
    vcol = (nq_cols + nk_cols) // d
    key_blk = lambda b, i, j: b * nq + jnp.maximum(i - j, 0)
    o_p = pl.pallas_call(
        functools.partial(_sb_attn_prompt_body, tq=tq),
        grid=(n_prompt_seqs, kvh, nq, nq),
        in_specs=[pl.BlockSpec((tq, grp * d), lambda b, g, i, j: (b * nq + i, g)),
                  pl.BlockSpec((tq, d), lambda b, g, i, j: (key_blk(b, i, j), kcol + g)),
                  pl.BlockSpec((tq, d), lambda b, g, i, j: (key_blk(b, i, j), vcol + g))],
        out_specs=pl.BlockSpec((tq, grp * d), lambda b, g, i, j: (b * nq + i, g)),
        out_shape=jax.ShapeDtypeStruct((mp, nq_cols), BF16),
        scratch_shapes=[pltpu.VMEM((grp * tq, 1), F32), pltpu.VMEM((grp * tq, d), F32)],
        compiler_params=_cp("parallel", "parallel", "parallel", "arbitrary"), name="sb_attn_prompt")(h, h, h)

    n_pages = page_table.shape[1]
    q_s = h[mp:, :nq_cols].reshape(ms, heads, d)
    carry0 = jnp.zeros((ms, heads, 1), F32)
    acc0 = jnp.zeros((ms, heads, d), F32)
    n_first = min(2, n_pages)
    carry1, acc1 = _sb_sample_pages(q_s, carry0, acc0, cache_k, cache_v, page_table, n_pages - 1, n_first, False)
    if n_pages > n_first:
        carry1, acc1 = lax.cond(
            jnp.max(carry1) > EXP_ZERO_BELOW,
            lambda c, a: tuple(_sb_sample_pages(q_s, c, a, cache_k, cache_v, page_table,
                                                n_pages - 1 - n_first, n_pages - n_first, True)),
            lambda c, a: (c, a), carry1, acc1)
    o_s = acc1.reshape(ms, nq_cols).astype(BF16)

    y = mm(jnp.concatenate([o_p, o_s], axis=0), w_o)
    return y, k_new, v_new


def _gla_log_decay(hg, wg, bg):
    pre = _dot(hg.astype(BF16), wg.astype(BF16)) + bg
    return -_softplus(-pre) / GLA_GATE_NORM


def _gla_prompt_body(q_ref, k_ref, v_ref, r_ref, hg_ref, wg_ref, bg_ref, ng_ref, o_ref, s_ref, *, n_chunks):
    n = pl.program_id(2)
    c = GLA_CHUNK
    dk = q_ref.shape[-1]

    @pl.when(n == 0)
    def _():
        s_ref[...] = jnp.zeros_like(s_ref)

    t_i = lax.broadcasted_iota(jnp.int32, (c, c), 0)
    s_i = lax.broadcasted_iota(jnp.int32, (c, c), 1)
    causal = s_i <= t_i
    lower = jnp.where(causal, 1.0, 0.0).astype(BF16)
    for ci in range(n_chunks):
        rows = slice(ci * c, (ci + 1) * c)
        la = _gla_log_decay(hg_ref[rows, :], wg_ref[...], bg_ref[...])
        hi, lo = _split_bf16(la)
        b = _dot(lower, hi) + _dot(lower, lo)
        b_last = b[c - 1:c, :]
        q = q_ref[rows, :] * (dk ** -0.5)
        k = k_ref[rows, :]
        v = v_ref[rows, :].astype(BF16)
        qe = (q * jnp.exp(b)).astype(BF16)
        ke = (k * jnp.exp(-b)).astype(BF16)
        kd = (k * jnp.exp(b_last - b)).astype(BF16)
        att = jnp.where(causal, _dot_nt(qe, ke), 0.0)
        state = s_ref[...]
        o = _dot(qe, state.astype(BF16)) + _dot(att.astype(BF16), v)
        decay = jnp.sum(jnp.transpose(la), axis=1, keepdims=True)
        s_ref[...] = jnp.exp(decay) * state + _dot_tn(kd, v)
        on = o * lax.rsqrt(jnp.mean(o * o, axis=-1, keepdims=True) + RMS_EPS) * ng_ref[...]
        r = r_ref[rows, :]
        o_ref[rows, :] = (on * (r * _sigmoid(r))).astype(o_ref.dtype)


def _gla_sample_body(q_ref, k_ref, v_ref, r_ref, hg_ref, wg_ref, bg_ref, ng_ref, s0_ref, o_ref, s_ref,
                     kt_ref, at_ref, qt_ref):
    b = pl.program_id(1)
    dk = q_ref.shape[-1]

    @pl.when(b == 0)
    def _():
        la = _gla_log_decay(hg_ref[...], wg_ref[...], bg_ref[...])
        a = jnp.exp(la)
        kt_ref[...] = jnp.transpose(k_ref[...])
        at_ref[...] = jnp.transpose(a)
        qt_ref[...] = jnp.transpose(q_ref[...] * (dk ** -0.5) * a)

    ms = kt_ref.shape[1]
    sel = lax.broadcasted_iota(jnp.int32, (1, ms), 1) == b
    col = lambda ref: jnp.sum(jnp.where(sel, ref[...], 0.0), axis=1, keepdims=True)
    k_col, a_col, qe_col = col(kt_ref), col(at_ref), col(qt_ref)
    v_row = v_ref[pl.ds(b, 1), :]
    s0 = s0_ref[...]
    att = jnp.sum(qe_col * (k_col / a_col), axis=0, keepdims=True)
    o = jnp.sum(qe_col * s0, axis=0, keepdims=True) + att * v_row
    s_ref[...] = a_col * s0 + k_col * v_row
    on = o * lax.rsqrt(jnp.mean(o * o, axis=-1, keepdims=True) + RMS_EPS) * ng_ref[...]
    r = r_ref[pl.ds(b, 1), :]
    o_ref[...] = (on * (r * _sigmoid(r))).astype(o_ref.dtype)


def gla_mixer(xb, n_prompt_seqs, seq, state_in, w_in, w_gate_up, b_gate, norm_g, w_o):
    m = xb.shape[0]
    mp = n_prompt_seqs * seq
    ms = m - mp
    heads = GLA_HEADS
    qk = w_gate_up.shape[1]
    vd = w_o.shape[0]
    dk, dv = qk // heads, vd // heads
    rank = w_gate_up.shape[0]
    h = mm(xb, w_in)
    hg = h[:, 2 * qk + 2 * vd:]
    bg = b_gate.reshape(1, qk)
    ng = norm_g.reshape(1, dv)

    ts = _tile(seq, 256, GLA_CHUNK)
    ns = seq // ts
    kb, vb, rb = qk // dk, 2 * qk // dv, (2 * qk + vd) // dv
    rows = lambda b, hd, n: b * ns + n
    o_p, s_p = pl.pallas_call(
        functools.partial(_gla_prompt_body, n_chunks=ts // GLA_CHUNK),
        grid=(n_prompt_seqs, heads, ns),
        in_specs=[pl.BlockSpec((ts, dk), lambda b, hd, n: (rows(b, hd, n), hd)),
                  pl.BlockSpec((ts, dk), lambda b, hd, n: (rows(b, hd, n), kb + hd)),
                  pl.BlockSpec((ts, dv), lambda b, hd, n: (rows(b, hd, n), vb + hd)),
                  pl.BlockSpec((ts, dv), lambda b, hd, n: (rows(b, hd, n), rb + hd)),
                  pl.BlockSpec((ts, rank), lambda b, hd, n: (rows(b, hd, n), 0)),
                  pl.BlockSpec((rank, dk), lambda b, hd, n: (0, hd)),
                  pl.BlockSpec((1, dk), lambda b, hd, n: (0, hd)),
                  pl.BlockSpec((1, dv), lambda b, hd, n: (0, 0))],
        out_specs=[pl.BlockSpec((ts, dv), lambda b, hd, n: (rows(b, hd, n), hd)),
                   pl.BlockSpec((None, None, dk, dv), lambda b, hd, n: (b, hd, 0, 0))],
        out_shape=[jax.ShapeDtypeStruct((mp, vd), BF16), jax.ShapeDtypeStruct((n_prompt_seqs, heads, dk, dv), F32)],
        compiler_params=_cp("parallel", "parallel", "arbitrary"), name="gla_prompt")(h, h, h, h, hg, w_gate_up, bg, ng)

    hs = h[mp:]
    o_s, s_s = pl.pallas_call(
        _gla_sample_body, grid=(heads, ms),
        in_specs=[pl.BlockSpec((ms, dk), lambda hd, b: (0, hd)),
                  pl.BlockSpec((ms, dk), lambda hd, b: (0, kb + hd)),
                  pl.BlockSpec((ms, dv), lambda hd, b: (0, vb + hd)),
                  pl.BlockSpec((ms, dv), lambda hd, b: (0, rb + hd)),
                  pl.BlockSpec((ms, rank), lambda hd, b: (0, 0)),
                  pl.BlockSpec((rank, dk), lambda hd, b: (0, hd)),
                  pl.BlockSpec((1, dk), lambda hd, b: (0, hd)),
                  pl.BlockSpec((1, dv), lambda hd, b: (0, 0)),
                  pl.BlockSpec((None, None, dk, dv), lambda hd, b: (b, hd, 0, 0))],
        out_specs=[pl.BlockSpec((None, 1, dv), lambda hd, b: (b, 0, hd)),
                   pl.BlockSpec((None, None, dk, dv), lambda hd, b: (b, hd, 0, 0))],
        out_shape=[jax.ShapeDtypeStruct((ms, 1, vd), BF16), jax.ShapeDtypeStruct((ms, heads, dk, dv), F32)],
        scratch_shapes=[pltpu.VMEM((dk, ms), F32)] * 3,
        compiler_params=_cp("arbitrary", "arbitrary"), name="gla_sample")(hs, hs, hs, hs, hg[mp:], w_gate_up, bg, ng, state_in)

    y = mm(jnp.concatenate([o_p, o_s.reshape(ms, vd)], axis=0), w_o)
    return y, s_p, s_s


def _ln_swish(y, g, b):
    mu = jnp.mean(y, axis=-1, keepdims=True)
    yc = y - mu
    var = jnp.mean(yc * yc, axis=-1, keepdims=True)
    yn = yc * lax.rsqrt(var + LN_EPS) * g + b
    return yn * _sigmoid(yn)


def _conv_prompt_body(u_ref, w_ref, bdw_ref, g_ref, b_ref, o_ref, ext_ref, y_ref, *, tq, pad, rb):
    t = pl.program_id(1)
    width = w_ref.shape[0]
    ch = u_ref.shape[1]
    lead = pad - (width - 1)

    @pl.when(t == 0)
    def _():
        ext_ref[0:pad, :] = jnp.zeros((pad, ch), F32)

    ext_ref[pad:pad + tq, :] = u_ref[...]

    def row_block(i, carry):
        r0 = pl.multiple_of(i * rb, rb)
        for c0 in range(0, ch, LANE):
            win = ext_ref[pl.ds(r0, rb + pad), c0:c0 + LANE]
            acc = jnp.zeros((rb, LANE), F32)
            for k in range(width):
                acc = acc + w_ref[k:k + 1, c0:c0 + LANE] * win[lead + k:lead + k + rb, :]
            y_ref[pl.ds(r0, rb), c0:c0 + LANE] = acc
        return carry

    lax.fori_loop(0, tq // rb, row_block, 0)
    ext_ref[0:pad, :] = ext_ref[tq:tq + pad, :]
    o_ref[...] = _ln_swish(y_ref[...] + bdw_ref[...], g_ref[...], b_ref[...]).astype(o_ref.dtype)


def _conv_sample_body(buf_ref, u_ref, w_ref, bdw_ref, g_ref, b_ref, o_ref, s_ref):
    width = w_ref.shape[0]
    buf = buf_ref[...]
    u = u_ref[...]
    y = jnp.sum(buf * w_ref[0:width - 1, :][None], axis=1) + u * w_ref[width - 1:width, :] + bdw_ref[...]
    o_ref[...] = _ln_swish(y, g_ref[...], b_ref[...]).astype(o_ref.dtype)
    s_ref[:, 0:width - 2, :] = buf[:, 1:width - 1, :]
    s_ref[:, width - 2:width - 1, :] = u[:, None, :]


def conv_mixer(xb, n_prompt_seqs, seq, state_in, w_pw1, b_pw1, w_dw, b_dw, ln_g, ln_b, w_pw2, b_pw2):
    m, d = xb.shape
    mp = n_prompt_seqs * seq
    ms = m - mp
    ch = w_dw.shape[1]
    width = w_dw.shape[0]
    u = glu(xb, w_pw1, ch, mode="glu", bias=b_pw1, out_dtype=F32)
    vec = lambda a: a.reshape(1, ch)

    tq = _tile(seq, 256, 64)
    nt = seq // tq
    pad, rb = 32, 64
    fixed = lambda shape: pl.BlockSpec(shape, lambda b, t: (0, 0))
    y_p = pl.pallas_call(
        functools.partial(_conv_prompt_body, tq=tq, pad=pad, rb=rb), grid=(n_prompt_seqs, nt),
        in_specs=[pl.BlockSpec((tq, ch), lambda b, t: (b * nt + t, 0)), fixed((width, ch)),
                  fixed((1, ch)), fixed((1, ch)), fixed((1, ch))],
        out_specs=pl.BlockSpec((tq, ch), lambda b, t: (b * nt + t, 0)),
        out_shape=jax.ShapeDtypeStruct((mp, ch), BF16),
        scratch_shapes=[pltpu.VMEM((tq + pad, ch), F32), pltpu.VMEM((tq, ch), F32)],
        compiler_params=_cp("parallel", "arbitrary"), name="conv_prompt")(u, w_dw, vec(b_dw), vec(ln_g), vec(ln_b))
    buf_p = u[:mp].reshape(n_prompt_seqs, seq, ch)[:, seq - (width - 1):]

    gs = _tile(ms, 16, 16)
    fixed1 = lambda shape: pl.BlockSpec(shape, lambda i: (0, 0))
    y_s, buf_s = pl.pallas_call(
        _conv_sample_body, grid=(ms // gs,),
        in_specs=[pl.BlockSpec((gs, width - 1, ch), lambda i: (i, 0, 0)), pl.BlockSpec((gs, ch), lambda i: (i, 0)),
                  fixed1((width, ch)), fixed1((1, ch)), fixed1((1, ch)), fixed1((1, ch))],
        out_specs=[pl.BlockSpec((gs, ch), lambda i: (i, 0)), pl.BlockSpec((gs, width - 1, ch), lambda i: (i, 0, 0))],
        out_shape=[jax.ShapeDtypeStruct((ms, ch), BF16), jax.ShapeDtypeStruct((ms, width - 1, ch), F32)],
        compiler_params=_cp("parallel"), name="conv_sample")(state_in, u[mp:], w_dw, vec(b_dw), vec(ln_g), vec(ln_b))

    y = mm(jnp.concatenate([y_p, y_s], axis=0), w_pw2)
    return y, b_pw2, buf_p, buf_s


def glu(xb, w, half, mode, bias=None, out_dtype=BF16, tn_target=512):
    m, k = xb.shape
    tm = _tile(m, 1040, 16)
    tn = _tile(half, tn_target, LANE)
    nb = half // tn
    in_specs = [pl.BlockSpec((tm, k), lambda i, j: (i, 0)), pl.BlockSpec((k, tn), lambda i, j: (0, j)),
                pl.BlockSpec((k, tn), lambda i, j: (0, nb + j))]
    args = [xb, w, w]
    if bias is not None:
        b2 = bias.reshape(1, 2 * half)
        in_specs += [pl.BlockSpec((1, tn), lambda i, j: (0, j)), pl.BlockSpec((1, tn), lambda i, j: (0, nb + j))]
        args += [b2, b2]
    return pl.pallas_call(
        functools.partial(_glu_body, mode=mode, has_bias=bias is not None, has_gate=False),
        grid=(m // tm, nb), in_specs=in_specs, out_specs=pl.BlockSpec((tm, tn), lambda i, j: (i, j)),
        out_shape=jax.ShapeDtypeStruct((m, half), out_dtype),
        compiler_params=_cp("parallel", "parallel"), name="glu")(*args)


def dense_ffn(xb, w_gu, w_down):
    m, d = xb.shape
    ff = w_down.shape[0]
    ffp = -(-ff // 512) * 512
    wg = jnp.pad(w_gu[:, :ff], ((0, 0), (0, ffp - ff)))
    wu = jnp.pad(w_gu[:, ff:], ((0, 0), (0, ffp - ff)))
    h = glu(xb, jnp.concatenate([wg, wu], axis=1).astype(BF16), ffp, mode="swiglu")
    wd = jnp.pad(w_down, ((0, ffp - ff), (0, 0)))
    tm, tn, tk = _tile(m, 1040, 16), 1024, 512
    return mm_acc(h, wd, lambda i, j, k: (i, k), lambda i, j, k: (k, j), (tm, tk), (tk, tn), ffp // tk, m, d, tm, tn)


def _router_body(x_ref, w_ref, g_ref):
    logits = jnp.dot(x_ref[...], w_ref[...], preferred_element_type=F32, precision=lax.Precision.HIGHEST)
    n_e = logits.shape[1]
    idx = lax.broadcasted_iota(jnp.int32, logits.shape, 1)
    m1 = jnp.max(logits, axis=1, keepdims=True)
    i1 = jnp.min(jnp.where(logits == m1, idx, n_e), axis=1, keepdims=True)
    rest = jnp.where(idx == i1, -jnp.inf, logits)
    m2 = jnp.max(rest, axis=1, keepdims=True)
    i2 = jnp.min(jnp.where(rest == m2, idx, n_e), axis=1, keepdims=True)
    e2 = jnp.exp(m2 - m1)
    w1 = 1.0 / (1.0 + e2)
    w2 = e2 / (1.0 + e2)
    g_ref[...] = jnp.where(idx == i1, w1, 0.0) + jnp.where(idx == i2, w2, 0.0)


def moe_ffn(x32, xb, w_router, w_gu, w_down, layer):
    m, d = xb.shape
    n_e = w_router.shape[-1]
    ffe = w_down.shape[2]
    tmr = _tile(m, 416, 8)
    gates = pl.pallas_call(
        _router_body, grid=(m // tmr,),
        in_specs=[pl.BlockSpec((tmr, d), lambda i: (i, 0)), pl.BlockSpec((None, d, n_e), lambda i: (layer, 0, 0))],
        out_specs=pl.BlockSpec((tmr, n_e), lambda i: (i, 0)),
        out_shape=jax.ShapeDtypeStruct((m, n_e), F32),
        compiler_params=_cp("parallel"), name="router")(x32, w_router)

    tm = _tile(m, 1040, 16)
    tn = _tile(ffe, 512, LANE)
    nb = ffe // tn
    h = pl.pallas_call(
        functools.partial(_glu_body, mode="swiglu", has_bias=False, has_gate=True),
        grid=(n_e, m // tm, nb),
        in_specs=[pl.BlockSpec((tm, d), lambda e, i, j: (i, 0)),
                  pl.BlockSpec((None, None, d, tn), lambda e, i, j: (layer, e, 0, j)),
                  pl.BlockSpec((None, None, d, tn), lambda e, i, j: (layer, e, 0, nb + j)),
                  pl.BlockSpec((tm, n_e), lambda e, i, j: (i, 0))],
        out_specs=pl.BlockSpec((None, tm, tn), lambda e, i, j: (e, i, j)),
        out_shape=jax.ShapeDtypeStruct((n_e, m, ffe), BF16),
        compiler_params=_cp("parallel", "parallel", "parallel"), name="moe_glu")(xb, w_gu, w_gu, gates)

    tk, tn2 = 512, 1024
    nk = ffe // tk
    return pl.pallas_call(
        _mm_acc_body, grid=(m // tm, d // tn2, n_e * nk),
        in_specs=[pl.BlockSpec((None, tm, tk), lambda i, j, k: (k // nk, i, k % nk)),
                  pl.BlockSpec((None, None, tk, tn2), lambda i, j, k: (layer, k // nk, k % nk, j))],
        out_specs=pl.BlockSpec((tm, tn2), lambda i, j, k: (i, j)),
        out_shape=jax.ShapeDtypeStruct((m, d), F32),
        compiler_params=_cp("parallel", "parallel", "arbitrary"), name="moe_down")(h, w_down)


@jax.jit
def _step(x_prompt, x_sample, cache_mla_latent, cache_mla_rope, cache_sb_k, cache_sb_v, state_gla, state_conv,
          page_table, ln_g, ln_b, mla_w_in, mla_q_norm, mla_kv_norm, mla_w_q_up, mla_w_kv_up, mla_w_o,
          sb_w_in, sb_w_o, gla_w_in, gla_w_gate_up, gla_b_gate, gla_norm, gla_w_o,
          conv_w_pw1, conv_b_pw1, conv_w_dw, conv_b_dw, conv_ln_g, conv_ln_b, conv_w_pw2, conv_b_pw2,
          ffn_w_gu, ffn_w_down, moe_w_router, moe_w_gu, moe_w_down):
    bsz, seq, d = x_prompt.shape
    dec = x_sample.shape[0]
    assert x_sample.shape[1] == 1, "one new token per sampled sequence"
    assert ln_g.shape[0] == DEPTH
    mp = bsz * seq
    past = page_table.shape[1] * cache_mla_latent.shape[2]
    x32 = jnp.concatenate([x_prompt.reshape(mp, d), x_sample.reshape(dec, d)], axis=0)
    xb = x32.astype(BF16)
    outs = {}
    for i in range(DEPTH):
        mixer = i % 4
        bias = None
        if mixer == 0:
            y, c32, r32 = mla_mixer(xb, bsz, seq, past, cache_mla_latent, cache_mla_rope, page_table,
                                    mla_w_in[0], mla_q_norm[0], mla_kv_norm[0], mla_w_q_up[0], mla_w_kv_up[0], mla_w_o[0])
            cr, rr = c32.shape[1], r32.shape[1]
            outs["mla"] = (c32[:mp].reshape(1, bsz, seq, cr), r32[:mp].reshape(1, bsz, seq, rr),
                           c32[mp:].reshape(1, dec, 1, cr), r32[mp:].reshape(1, dec, 1, rr))
        elif mixer == 1:
            y, k_new, v_new = sb_mixer(xb, bsz, seq, cache_sb_k, cache_sb_v, page_table, sb_w_in[0], sb_w_o[0])
            kv = lambda a, lo, hi, b, t: a[lo:hi].reshape(1, b, t, SB_KV_HEADS, SB_HEAD_DIM)
            outs["sb"] = (kv(k_new, 0, mp, bsz, seq), kv(v_new, 0, mp, bsz, seq),
                          kv(k_new, mp, mp + dec, dec, 1), kv(v_new, mp, mp + dec, dec, 1))
        elif mixer == 2:
            y, s_p, s_s = gla_mixer(xb, bsz, seq, state_gla[0], gla_w_in[0], gla_w_gate_up[0], gla_b_gate[0],
                                    gla_norm[0], gla_w_o[0])
            outs["gla"] = (s_p[None], s_s[None])
        else:
            y, bias, buf_p, buf_s = conv_mixer(xb, bsz, seq, state_conv[0], conv_w_pw1[0], conv_b_pw1[0], conv_w_dw[0],
                                               conv_b_dw[0], conv_ln_g[0], conv_ln_b[0], conv_w_pw2[0], conv_b_pw2[0])
            outs["conv"] = (buf_p[None], buf_s[None])
        x32, xb = res_ln(x32, y, ln_g[i, 0], ln_b[i, 0], bias=bias)
        f = i // 2
        if i % 2 == 0:
            y = dense_ffn(xb, ffn_w_gu[f], ffn_w_down[f])
        else:
            y = moe_ffn(x32, xb, moe_w_router, moe_w_gu, moe_w_down, f)
        x32, xb = res_ln(x32, y, ln_g[i, 1], ln_b[i, 1])
    return (x32[:mp].reshape(bsz, seq, d), x32[mp:].reshape(dec, 1, d)) + outs["mla"] + outs["sb"] + outs["gla"] + outs["conv"]


def kernel(x_prompt, x_sample, cache_mla_latent, cache_mla_rope, cache_sb_k, cache_sb_v, state_gla, state_conv, page_table, ln_g, ln_b, mla_w_in, mla_q_norm, mla_kv_norm, mla_w_q_up, mla_w_kv_up, mla_w_o, sb_w_in, sb_w_o, gla_w_in, gla_w_gate_up, gla_b_gate, gla_norm, gla_w_o, conv_w_pw1, conv_b_pw1, conv_w_dw, conv_b_dw, conv_ln_g, conv_ln_b, conv_w_pw2, conv_b_pw2, ffn_w_gu, ffn_w_down, moe_w_router, moe_w_gu, moe_w_down):
    return _step(x_prompt, x_sample, cache_mla_latent, cache_mla_rope, cache_sb_k, cache_sb_v, state_gla, state_conv,
                 page_table, ln_g, ln_b, mla_w_in, mla_q_norm, mla_kv_norm, mla_w_q_up, mla_w_kv_up, mla_w_o,
                 sb_w_in, sb_w_o, gla_w_in, gla_w_gate_up, gla_b_gate, gla_norm, gla_w_o,
                 conv_w_pw1, conv_b_pw1, conv_w_dw, conv_b_dw, conv_ln_g, conv_ln_b, conv_w_pw2, conv_b_pw2,
                 ffn_w_gu, ffn_w_down, moe_w_router, moe_w_gu, moe_w_down)
```

```python
import functools

import jax
import jax.numpy as jnp
from jax import lax
from jax.experimental import pallas as pl
from jax.experimental.pallas import tpu as pltpu

F32 = jnp.float32
BF16 = jnp.bfloat16

DEPTH = 4
DEEPNORM_ALPHA = (2.0 * DEPTH) ** 0.25
LN_EPS = 1e-5
RMS_EPS = 1e-6
MLA_HEADS = 16
MLA_NOPE = 128
MLA_ROPE = 64
MLA_V = 128
MLA_SCALE = (MLA_NOPE + MLA_ROPE) ** -0.5
ROPE_THETA = 10000.0
SB_HEADS = 16
SB_KV_HEADS = 4
SB_HEAD_DIM = 128
SB_GROUP = SB_HEADS // SB_KV_HEADS
SB_SCALE = SB_HEAD_DIM ** -0.5
GLA_HEADS = 4
GLA_GATE_NORM = 16.0
GLA_CHUNK = 64
CONV_WIDTH = 31
TOP_K = 2

V7X_VMEM_LIMIT = 56 * 1024 * 1024
EXP_ZERO_BELOW = -104.0
LANE = 128


def _cp(*sem):
    return pltpu.CompilerParams(dimension_semantics=sem, vmem_limit_bytes=V7X_VMEM_LIMIT)


def _tile(n, target, mult):
    best = None
    for t in range(mult, min(n, target) + 1, mult):
        if n % t == 0:
            best = t
    return best if best is not None else n


def _sigmoid(x):
    return 1.0 / (1.0 + jnp.exp(-x))


def _softplus(x):
    return jnp.maximum(x, 0.0) + jnp.log(1.0 + jnp.exp(-jnp.abs(x)))


def _split_bf16(x):
    hi = x.astype(BF16)
    lo = (x - hi.astype(F32)).astype(BF16)
    return hi, lo


def _dot(a, b):
    return jnp.dot(a, b, preferred_element_type=F32)


def _dot_nt(a, b):
    return lax.dot_general(a, b, (((1,), (1,)), ((), ())), preferred_element_type=F32)


def _dot_tn(a, b):
    return lax.dot_general(a, b, (((0,), (0,)), ((), ())), preferred_element_type=F32)


def _mm_body(*refs, has_bias):
    if has_bias:
        x_ref, w_ref, b_ref, o_ref = refs
    else:
        x_ref, w_ref, o_ref = refs
    acc = _dot(x_ref[...].astype(BF16), w_ref[...].astype(BF16))
    if has_bias:
        acc = acc + b_ref[...]
    o_ref[...] = acc.astype(o_ref.dtype)


def mm(x, w, bias=None, out_dtype=F32, tm_target=1040, tn_target=512):
    m, k = x.shape
    n = w.shape[1]
    tm = _tile(m, tm_target, 16)
    tn = n if n <= tn_target else tn_target
    grid = (m // tm, pl.cdiv(n, tn))
    in_specs = [pl.BlockSpec((tm, k), lambda i, j: (i, 0)), pl.BlockSpec((k, tn), lambda i, j: (0, j))]
    args = [x, w]
    if bias is not None:
        in_specs.append(pl.BlockSpec((1, tn), lambda i, j: (0, j)))
        args.append(bias.reshape(1, n))
    return pl.pallas_call(
        functools.partial(_mm_body, has_bias=bias is not None),
        grid=grid, in_specs=in_specs, out_specs=pl.BlockSpec((tm, tn), lambda i, j: (i, j)),
        out_shape=jax.ShapeDtypeStruct((m, n), out_dtype),
        compiler_params=_cp("parallel", "parallel"), name="mm")(*args)


def _mm_acc_body(x_ref, w_ref, o_ref):
    @pl.when(pl.program_id(2) == 0)
    def _():
        o_ref[...] = jnp.zeros_like(o_ref)

    o_ref[...] += _dot(x_ref[...], w_ref[...].astype(BF16))


def mm_acc(x, w, x_map, w_map, x_block, w_block, n_k, m, n, tm, tn):
    return pl.pallas_call(
        _mm_acc_body, grid=(m // tm, n // tn, n_k),
        in_specs=[pl.BlockSpec(x_block, x_map), pl.BlockSpec(w_block, w_map)],
        out_specs=pl.BlockSpec((tm, tn), lambda i, j, k: (i, j)),
        out_shape=jax.ShapeDtypeStruct((m, n), F32),
        compiler_params=_cp("parallel", "parallel", "arbitrary"), name="mm_acc")(x, w)


def _glu_body(*refs, mode, has_bias):
    refs = list(refs)
    x_ref, wa_ref, wb_ref = refs[:3]
    o_ref = refs[-1]
    x = x_ref[...]
    a = _dot(x, wa_ref[...].astype(BF16))
    b = _dot(x, wb_ref[...].astype(BF16))
    if has_bias:
        a = a + refs[3][...]
        b = b + refs[4][...]
    if mode == "swiglu":
        h = a * _sigmoid(a) * b
    else:
        h = a * _sigmoid(b)
    o_ref[...] = h.astype(o_ref.dtype)


def _res_ln_body(*refs, has_bias, two_terms):
    refs = list(refs)
    x_ref, y_ref = refs[:2]
    g_ref, b_ref, o32_ref, o16_ref = refs[-4:]
    y = y_ref[...]
    if two_terms:
        y = y + refs[2][...]
    z = DEEPNORM_ALPHA * x_ref[...] + y
    if has_bias:
        bias_ref = refs[2]
        z = z + bias_ref[...]
    mu = jnp.mean(z, axis=-1, keepdims=True)
    zc = z - mu
    var = jnp.mean(zc * zc, axis=-1, keepdims=True)
    out = zc * lax.rsqrt(var + LN_EPS) * g_ref[...] + b_ref[...]
    o32_ref[...] = out
    o16_ref[...] = out.astype(BF16)


def res_ln(x, y, g, b, bias=None):
    m, d = x.shape
    tm = _tile(m, 416, 16)
    two_terms = y.shape[0] == 2 * m
    assert not (two_terms and bias is not None)
    row = pl.BlockSpec((tm, d), lambda i: (i, 0))
    vec = pl.BlockSpec((1, d), lambda i: (0, 0))
    args = [x, y] + ([bias.reshape(1, d)] if bias is not None else []) + ([y] if two_terms else [])
    args += [g.reshape(1, d), b.reshape(1, d)]
    in_specs = [row, row] + ([vec] if bias is not None else [])
    in_specs += ([pl.BlockSpec((tm, d), lambda i: (i + m // tm, 0))] if two_terms else []) + [vec, vec]
    return pl.pallas_call(
        functools.partial(_res_ln_body, has_bias=bias is not None, two_terms=two_terms),
        grid=(m // tm,), in_specs=in_specs, out_specs=[row, row],
        out_shape=[jax.ShapeDtypeStruct((m, d), F32), jax.ShapeDtypeStruct((m, d), BF16)],
        compiler_params=_cp("parallel"), name="res_ln")(*args)


def _rope_tables(pos, half):
    inv_freq = ROPE_THETA ** (-jnp.arange(half, dtype=F32) / half)
    ang = pos.astype(F32)[:, None] * inv_freq[None, :]
    return jnp.cos(ang), jnp.sin(ang)


def _mla_a_body(h_ref, cos_ref, sin_ref, qg_ref, kvg_ref, c32_ref, c16_ref, r32_ref, r16_ref, qn_ref, *, qr, cr):
    h = h_ref[...]
    qa = h[:, :qr]
    ckv = h[:, qr:qr + cr]
    kr = h[:, qr + cr:]
    qn = qa * lax.rsqrt(jnp.mean(qa * qa, axis=-1, keepdims=True) + RMS_EPS) * qg_ref[...]
    c = ckv * lax.rsqrt(jnp.mean(ckv * ckv, axis=-1, keepdims=True) + RMS_EPS) * kvg_ref[...]
    half = kr.shape[1] // 2
    x1, x2 = kr[:, :half], kr[:, half:]
    cos, sin = cos_ref[...], sin_ref[...]
    r = jnp.concatenate([x1 * cos - x2 * sin, x2 * cos + x1 * sin], axis=-1)
    qn_ref[...] = qn.astype(BF16)
    c32_ref[...] = c
    c16_ref[...] = c.astype(BF16)
    r32_ref[...] = r
    r16_ref[...] = r.astype(BF16)


def _mla_c_body(q_ref, cos_ref, sin_ref, wuk_ref, qlat_ref, qpe_ref, *, heads, nope, rope):
    cos, sin = cos_ref[...], sin_ref[...]
    half = rope // 2
    for h in range(heads):
        qn = q_ref[:, h * nope:(h + 1) * nope].astype(BF16)
        qlat_ref[h] = (_dot(qn, wuk_ref[h]) * MLA_SCALE).astype(BF16)
        base = heads * nope + h * rope
        blk = q_ref[:, base // LANE * LANE:base // LANE * LANE + LANE]
        off = base % LANE
        x1 = blk[:, off:off + half]
        x2 = blk[:, off + half:off + rope]
        pe = jnp.concatenate([x1 * cos - x2 * sin, x2 * cos + x1 * sin], axis=-1)
        qpe_ref[h] = (pe * MLA_SCALE).astype(BF16)


def _mla_attn_prompt_body(qlat_ref, qpe_ref, c_ref, r_ref, wuv_ref, o_ref, m_ref, l_ref, acc_ref, *, heads, tq):
    qi = pl.program_id(1)
    kj = pl.program_id(2)
    nk = pl.num_programs(2)
    rows = heads * tq

    @pl.when(kj == 0)
    def _():
        m_ref[...] = jnp.full_like(m_ref, -jnp.inf)
        l_ref[...] = jnp.zeros_like(l_ref)
        acc_ref[...] = jnp.zeros_like(acc_ref)

    def step(diag):
        q = qlat_ref[...].reshape(rows, qlat_ref.shape[-1])
        qp = qpe_ref[...].reshape(rows, qpe_ref.shape[-1])
        c = c_ref[...]
        s = _dot_nt(q, c) + _dot_nt(qp, r_ref[...])
        tk = s.shape[-1]
        if diag:
            s3 = s.reshape(heads, tq, tk)
            qpos = lax.broadcasted_iota(jnp.int32, (1, tq, tk), 1)
            kpos = lax.broadcasted_iota(jnp.int32, (1, tq, tk), 2)
            s = jnp.where(kpos <= qpos, s3, -jnp.inf).reshape(rows, tk)
        m_old = m_ref[...]
        m_new = jnp.maximum(m_old, jnp.max(s, axis=-1, keepdims=True))
        p = jnp.exp(s - m_new)
        alpha = jnp.exp(m_old - m_new)
        l_ref[...] = alpha * l_ref[...] + jnp.sum(p, axis=-1, keepdims=True)
        acc_ref[...] = alpha * acc_ref[...] + _dot(p.astype(BF16), c)
        m_ref[...] = m_new

    @pl.when(kj < qi)
    def _():
        step(False)

    @pl.when(kj == qi)
    def _():
        step(True)

    @pl.when(kj == nk - 1)
    def _():
        v = MLA_V
        for h in range(heads):
            lat = acc_ref[h * tq:(h + 1) * tq, :] / l_ref[h * tq:(h + 1) * tq, :]
            o_ref[:, h * v:(h + 1) * v] = _dot(lat.astype(BF16), wuv_ref[h]).astype(o_ref.dtype)


def _mla_attn_sample_body(pt_ref, *refs, heads, n_group):
    del pt_ref
    qlat_ref, qpe_ref, cnew_ref, rnew_ref = refs[:4]
    c_refs = refs[4:4 + n_group]
    r_refs = refs[4 + n_group:4 + 2 * n_group]
    o_ref = refs[4 + 2 * n_group]
    m_ref, l_ref, acc_ref = refs[4 + 2 * n_group + 1:]
    j = pl.program_id(1)
    nj = pl.num_programs(1)

    @pl.when(j == 0)
    def _():
        m_ref[...] = jnp.full_like(m_ref, -jnp.inf)
        l_ref[...] = jnp.zeros_like(l_ref)
        acc_ref[...] = jnp.zeros_like(acc_ref)

    q_pad = qlat_ref[...]
    qp_pad = qpe_ref[...]
    q = q_pad[:heads]
    qp = qp_pad[:heads]
    c = jnp.concatenate([c_refs[g][...].astype(BF16) for g in range(n_group)], axis=0)
    r = jnp.concatenate([r_refs[g][...].astype(BF16) for g in range(n_group)], axis=0)
    s = jnp.transpose(_dot_nt(c, q_pad) + _dot_nt(r, qp_pad))[:heads]
    m_old = m_ref[...]
    m_new = jnp.maximum(m_old, jnp.max(s, axis=-1, keepdims=True))
    p = jnp.exp(s - m_new)
    alpha = jnp.exp(m_old - m_new)
    l_ref[...] = alpha * l_ref[...] + jnp.sum(p, axis=-1, keepdims=True)
    acc_ref[...] = alpha * acc_ref[...] + _dot(p.astype(BF16), c)
    m_ref[...] = m_new

    @pl.when(j == nj - 1)
    def _():
        cn = cnew_ref[...].astype(F32)
        s = (jnp.sum(q.astype(F32) * cn, axis=-1, keepdims=True)
             + jnp.sum(qp.astype(F32) * rnew_ref[...].astype(F32), axis=-1, keepdims=True))
        m_old = m_ref[...]
        m_new = jnp.maximum(m_old, s)
        p = jnp.exp(s - m_new)
        alpha = jnp.exp(m_old - m_new)
        l = alpha * l_ref[...] + p
        acc = alpha * acc_ref[...] + p.astype(BF16).astype(F32) * cn
        o_ref[...] = (acc / l).astype(o_ref.dtype)


def _mla_out_sample_body(x_ref, wuv_ref, o_ref, *, heads):
    v = MLA_V
    for h in range(heads):
        o_ref[:, h * v:(h + 1) * v] = _dot(x_ref[h], wuv_ref[h]).astype(o_ref.dtype)


def mla_mixer(xb, n_prompt_seqs, seq, past, cache_lat, cache_rope, page_table,
              w_in, q_norm, kv_norm, w_q_up, w_kv_up, w_o):
    m = xb.shape[0]
    mp = n_prompt_seqs * seq
    ms = m - mp
    qr, cr = q_norm.shape[0], kv_norm.shape[0]
    heads, nope, rope, vdim = MLA_HEADS, MLA_NOPE, MLA_ROPE, MLA_V
    half = rope // 2

    pos = jnp.concatenate([jnp.tile(jnp.arange(seq, dtype=jnp.int32), n_prompt_seqs),
                           jnp.full((ms,), past, dtype=jnp.int32)])
    cos, sin = _rope_tables(pos, half)

    h = mm(xb, w_in)
    tm = _tile(m, 416, 16)
    row = lambda w: pl.BlockSpec((tm, w), lambda i: (i, 0))
    vec = lambda w: pl.BlockSpec((1, w), lambda i: (0, 0))
    c32, c16, r32, r16, qn = pl.pallas_call(
        functools.partial(_mla_a_body, qr=qr, cr=cr), grid=(m // tm,),
        in_specs=[row(qr + cr + rope), row(half), row(half), vec(qr), vec(cr)],
        out_specs=[row(cr), row(cr), row(rope), row(rope), row(qr)],
        out_shape=[jax.ShapeDtypeStruct((m, cr), F32), jax.ShapeDtypeStruct((m, cr), BF16),
                   jax.ShapeDtypeStruct((m, rope), F32), jax.ShapeDtypeStruct((m, rope), BF16),
                   jax.ShapeDtypeStruct((m, qr), BF16)],
        compiler_params=_cp("parallel"), name="mla_a")(h, cos, sin, q_norm.reshape(1, qr), kv_norm.reshape(1, cr))

    wq = w_q_up.reshape(qr, heads, nope + rope)
    wq = jnp.concatenate([wq[:, :, :nope].reshape(qr, heads * nope), wq[:, :, nope:].reshape(qr, heads * rope)], axis=1)
    q = mm(qn, wq)
    wuk = jnp.transpose(w_kv_up[:, :, :nope], (1, 2, 0)).astype(BF16)
    wuv = jnp.transpose(w_kv_up[:, :, nope:], (1, 0, 2)).astype(BF16)
    tm2 = _tile(m, 208, 16)
    qlat, qpe = pl.pallas_call(
        functools.partial(_mla_c_body, heads=heads, nope=nope, rope=rope), grid=(m // tm2,),
        in_specs=[pl.BlockSpec((tm2, heads * (nope + rope)), lambda i: (i, 0)),
                  pl.BlockSpec((tm2, half), lambda i: (i, 0)), pl.BlockSpec((tm2, half), lambda i: (i, 0)),
                  pl.BlockSpec((heads, nope, cr), lambda i: (0, 0, 0))],
        out_specs=[pl.BlockSpec((heads, tm2, cr), lambda i: (0, i, 0)),
                   pl.BlockSpec((heads, tm2, rope), lambda i: (0, i, 0))],
        out_shape=[jax.ShapeDtypeStruct((heads, m, cr), BF16), jax.ShapeDtypeStruct((heads, m, rope), BF16)],
        compiler_params=_cp("parallel"), name="mla_c")(q, cos, sin, wuk)

    tq = _tile(seq, 256, 16)
    nq = seq // tq
    o_p = pl.pallas_call(
        functools.partial(_mla_attn_prompt_body, heads=heads, tq=tq),
        grid=(n_prompt_seqs, nq, nq),
        in_specs=[pl.BlockSpec((heads, tq, cr), lambda b, i, j: (0, b * nq + i, 0)),
                  pl.BlockSpec((heads, tq, rope), lambda b, i, j: (0, b * nq + i, 0)),
                  pl.BlockSpec((tq, cr), lambda b, i, j: (b * nq + jnp.minimum(i, j), 0)),
                  pl.BlockSpec((tq, rope), lambda b, i, j: (b * nq + jnp.minimum(i, j), 0)),
                  pl.BlockSpec((heads, cr, vdim), lambda b, i, j: (0, 0, 0))],
        out_specs=pl.BlockSpec((tq, heads * vdim), lambda b, i, j: (b * nq + i, 0)),
        out_shape=jax.ShapeDtypeStruct((mp, heads * vdim), BF16),
        scratch_shapes=[pltpu.VMEM((heads * tq, 1), F32), pltpu.VMEM((heads * tq, 1), F32),
                        pltpu.VMEM((heads * tq, cr), F32)],
        compiler_params=_cp("parallel", "parallel", "arbitrary"), name="mla_attn_prompt")(qlat, qpe, c16, r16, wuv)

    n_pages = page_table.shape[1]
    page = cache_lat.shape[2]
    n_group = _tile(n_pages, 16, 1)
    pad_rows = ((0, 0), (0, LANE - heads), (0, 0))
    qlat_s = jnp.pad(jnp.transpose(qlat[:, mp:], (1, 0, 2)), pad_rows)
    qpe_s = jnp.pad(jnp.transpose(qpe[:, mp:], (1, 0, 2)), pad_rows)
    cnew = c16[mp:].reshape(ms, 1, cr)
    rnew = r16[mp:].reshape(ms, 1, rope)

    def page_map(g):
        return lambda b, j, pt: (0, pt[b * n_pages + j * n_group + g], 0, 0)

    seq_map = lambda b, j, pt: (b, 0, 0)
    lat_s = pl.pallas_call(
        functools.partial(_mla_attn_sample_body, heads=heads, n_group=n_group),
        grid_spec=pltpu.PrefetchScalarGridSpec(
            num_scalar_prefetch=1, grid=(ms, n_pages // n_group),
            in_specs=[pl.BlockSpec((None, LANE, cr), seq_map), pl.BlockSpec((None, LANE, rope), seq_map),
                      pl.BlockSpec((None, 1, cr), seq_map), pl.BlockSpec((None, 1, rope), seq_map)]
                     + [pl.BlockSpec((None, None, page, cr), page_map(g)) for g in range(n_group)]
                     + [pl.BlockSpec((None, None, page, rope), page_map(g)) for g in range(n_group)],
            out_specs=pl.BlockSpec((None, heads, cr), seq_map),
            scratch_shapes=[pltpu.VMEM((heads, 1), F32), pltpu.VMEM((heads, 1), F32), pltpu.VMEM((heads, cr), F32)]),
        out_shape=jax.ShapeDtypeStruct((ms, heads, cr), BF16),
        compiler_params=_cp("parallel", "arbitrary"), name="mla_attn_sample")(
            page_table.reshape(-1), qlat_s, qpe_s, cnew, rnew,
            *([cache_lat] * n_group), *([cache_rope] * n_group))
    o_s = pl.pallas_call(
        functools.partial(_mla_out_sample_body, heads=heads), grid=(1,),
        in_specs=[pl.BlockSpec((heads, ms, cr), lambda i: (0, 0, 0)), pl.BlockSpec((heads, cr, vdim), lambda i: (0, 0, 0))],
        out_specs=pl.BlockSpec((ms, heads * vdim), lambda i: (0, 0)),
        out_shape=jax.ShapeDtypeStruct((ms, heads * vdim), BF16),
        compiler_params=_cp("arbitrary"), name="mla_out_sample")(jnp.transpose(lat_s, (1, 0, 2)), wuv)

    y = mm(jnp.concatenate([o_p, o_s], axis=0), w_o)
    return y, c32, r32


def _sb_weights(z, carry, u, mask):
    sp = _softplus(z)
    lk = -sp if mask is None else jnp.where(mask, -sp, 0.0)
    hi, lo = _split_bf16(lk)
    rest = _dot(hi, u) + _dot(lo, u)
    a = jnp.exp(z - sp + rest + carry)
    if mask is not None:
        a = jnp.where(mask, a, 0.0)
    return a, carry + jnp.sum(lk, axis=-1, keepdims=True)


def _later_key_matrix(n):
    j = lax.broadcasted_iota(jnp.int32, (n, n), 0)
    s = lax.broadcasted_iota(jnp.int32, (n, n), 1)
    return jnp.where(j > s, 1.0, 0.0).astype(BF16)


def _sb_attn_prompt_body(q_ref, k_ref, v_ref, o_ref, carry_ref, acc_ref, *, tq):
    qi = pl.program_id(2)
    kj = pl.program_id(3)
    nk = pl.num_programs(3)
    d = SB_HEAD_DIM
    grp = SB_GROUP

    @pl.when(kj == 0)
    def _():
        carry_ref[...] = jnp.zeros_like(carry_ref)
        acc_ref[...] = jnp.zeros_like(acc_ref)

    def step(diag):
        qb = q_ref[...]
        q4 = jnp.concatenate([qb[:, g * d:(g + 1) * d] for g in range(grp)], axis=0).astype(BF16)
        k = k_ref[...].astype(BF16)
        z = _dot_nt(q4, k) * SB_SCALE
        tk = z.shape[-1]
        mask = None
        if diag:
            r = lax.broadcasted_iota(jnp.int32, (grp, tq, tk), 1).reshape(grp * tq, tk)
            c = lax.broadcasted_iota(jnp.int32, (grp * tq, tk), 1)
            mask = c < r
        a, carry = _sb_weights(z, carry_ref[...], _later_key_matrix(tk), mask)
        acc_ref[...] += _dot(a.astype(BF16), v_ref[...].astype(BF16))
        carry_ref[...] = carry

    @pl.when(kj == 0)
    def _():
        step(True)

    @pl.when(jnp.logical_and(kj > 0, kj <= qi))
    def _():
        @pl.when(jnp.max(carry_ref[...]) > EXP_ZERO_BELOW)
        def _():
            step(False)

    @pl.when(kj == nk - 1)
    def _():
        for g in range(grp):
            o_ref[:, g * d:(g + 1) * d] = acc_ref[g * tq:(g + 1) * tq, :].astype(o_ref.dtype)


def _sb_attn_sample_body(pt_ref, q_ref, cin_ref, ain_ref, k_ref, v_ref, cout_ref, aout_ref, *, check):
    del pt_ref
    j = pl.program_id(1)
    grp = SB_GROUP

    @pl.when(j == 0)
    def _():
        cout_ref[...] = cin_ref[...]
        aout_ref[...] = ain_ref[...]

    def step():
        q = q_ref[...].astype(BF16)
        page = k_ref.shape[0]
        z = jnp.concatenate(
            [_dot_nt(q[h * grp:(h + 1) * grp], k_ref[:, h, :].astype(BF16)) for h in range(SB_KV_HEADS)], axis=0) * SB_SCALE
        a, carry = _sb_weights(z, cout_ref[...], _later_key_matrix(page), None)
        ab = a.astype(BF16)
        for h in range(SB_KV_HEADS):
            aout_ref[h * grp:(h + 1) * grp, :] += _dot(ab[h * grp:(h + 1) * grp], v_ref[:, h, :].astype(BF16))
        cout_ref[...] = carry

    if check:
        @pl.when(jnp.max(cout_ref[...]) > EXP_ZERO_BELOW)
        def _():
            step()
    else:
        step()


def _sb_sample_pages(q_s, carry, acc, cache_k, cache_v, page_table, first, count, check):
    ms, heads, d = q_s.shape
    n_pages = page_table.shape[1]
    page = cache_k.shape[2]
    kvh = cache_k.shape[3]
    seq_map = lambda b, j, pt: (b, 0, 0)
    page_map = lambda b, j, pt: (0, pt[b * n_pages + first - j], 0, 0, 0)
    return pl.pallas_call(
        functools.partial(_sb_attn_sample_body, check=check),
        grid_spec=pltpu.PrefetchScalarGridSpec(
            num_scalar_prefetch=1, grid=(ms, count),
            in_specs=[pl.BlockSpec((None, heads, d), seq_map), pl.BlockSpec((None, heads, 1), seq_map),
                      pl.BlockSpec((None, heads, d), seq_map),
                      pl.BlockSpec((None, None, page, kvh, d), page_map),
                      pl.BlockSpec((None, None, page, kvh, d), page_map)],
            out_specs=[pl.BlockSpec((None, heads, 1), seq_map), pl.BlockSpec((None, heads, d), seq_map)]),
        out_shape=[jax.ShapeDtypeStruct((ms, heads, 1), F32), jax.ShapeDtypeStruct((ms, heads, d), F32)],
        compiler_params=_cp("parallel", "arbitrary"), name="sb_attn_sample")(
            page_table.reshape(-1), q_s, carry, acc, cache_k, cache_v)


def sb_mixer(xb, n_prompt_seqs, seq, cache_k, cache_v, page_table, w_in, w_o):
    m = xb.shape[0]
    mp = n_prompt_seqs * seq
    ms = m - mp
    heads, kvh, d, grp = SB_HEADS, SB_KV_HEADS, SB_HEAD_DIM, SB_GROUP
    nq_cols = heads * d
    nk_cols = kvh * d
    h = mm(xb, w_in)
    k_new = h[:, nq_cols:nq_cols + nk_cols]
    v_new = h[:, nq_cols + nk_cols:]

    tq = _tile(seq, 256, 8)
    nq = seq // tq
    kcol = nq_cols // d
    vcol = (nq_cols + nk_cols) // d
    key_blk = lambda b, i, j: b * nq + jnp.maximum(i - j, 0)
    o_p = pl.pallas_call(
        functools.partial(_sb_attn_prompt_body, tq=tq),
        grid=(n_prompt_seqs, kvh, nq, nq),
        in_specs=[pl.BlockSpec((tq, grp * d), lambda b, g, i, j: (b * nq + i, g)),
                  pl.BlockSpec((tq, d), lambda b, g, i, j: (key_blk(b, i, j), kcol + g)),
                  pl.BlockSpec((tq, d), lambda b, g, i, j: (key_blk(b, i, j), vcol + g))],
        out_specs=pl.BlockSpec((tq, grp * d), lambda b, g, i, j: (b * nq + i, g)),
        out_shape=jax.ShapeDtypeStruct((mp, nq_cols), BF16),
        scratch_shapes=[pltpu.VMEM((grp * tq, 1), F32), pltpu.VMEM((grp * tq, d), F32)],
        compiler_params=_cp("parallel", "parallel", "parallel", "arbitrary"), name="sb_attn_prompt")(h, h, h)

    n_pages = page_table.shape[1]
    q_s = h[mp:, :nq_cols].reshape(ms, heads, d)
    carry0 = jnp.zeros((ms, heads, 1), F32)
    acc0 = jnp.zeros((ms, heads, d), F32)
    n_first = min(2, n_pages)
    carry1, acc1 = _sb_sample_pages(q_s, carry0, acc0, cache_k, cache_v, page_table, n_pages - 1, n_first, False)
    if n_pages > n_first:
        carry1, acc1 = lax.cond(
            jnp.max(carry1) > EXP_ZERO_BELOW,
            lambda c, a: tuple(_sb_sample_pages(q_s, c, a, cache_k, cache_v, page_table,
                                                n_pages - 1 - n_first, n_pages - n_first, True)),
            lambda c, a: (c, a), carry1, acc1)
    o_s = acc1.reshape(ms, nq_cols).astype(BF16)

    y = mm(jnp.concatenate([o_p, o_s], axis=0), w_o)
    return y, k_new, v_new


def _gla_log_decay(hg, wg, bg):
    pre = _dot(hg.astype(BF16), wg.astype(BF16)) + bg
    return -_softplus(-pre) / GLA_GATE_NORM


def _gla_prompt_body(q_ref, k_ref, v_ref, r_ref, hg_ref, wg_ref, bg_ref, ng_ref, o_ref, s_ref, *, n_chunks):
    n = pl.program_id(2)
    c = GLA_CHUNK
    dk = q_ref.shape[-1]

    @pl.when(n == 0)
    def _():
        s_ref[...] = jnp.zeros_like(s_ref)

    t_i = lax.broadcasted_iota(jnp.int32, (c, c), 0)
    s_i = lax.broadcasted_iota(jnp.int32, (c, c), 1)
    causal = s_i <= t_i
    lower = jnp.where(causal, 1.0, 0.0).astype(BF16)
    for ci in range(n_chunks):
        rows = slice(ci * c, (ci + 1) * c)
        la = _gla_log_decay(hg_ref[rows, :], wg_ref[...], bg_ref[...])
        hi, lo = _split_bf16(la)
        b = _dot(lower, hi) + _dot(lower, lo)
        b_last = b[c - 1:c, :]
        q = q_ref[rows, :] * (dk ** -0.5)
        k = k_ref[rows, :]
        v = v_ref[rows, :].astype(BF16)
        qe = (q * jnp.exp(b)).astype(BF16)
        ke = (k * jnp.exp(-b)).astype(BF16)
        kd = (k * jnp.exp(b_last - b)).astype(BF16)
        att = jnp.where(causal, _dot_nt(qe, ke), 0.0)
        state = s_ref[...]
        o = _dot(qe, state.astype(BF16)) + _dot(att.astype(BF16), v)
        decay = jnp.sum(jnp.transpose(la), axis=1, keepdims=True)
        s_ref[...] = jnp.exp(decay) * state + _dot_tn(kd, v)
        on = o * lax.rsqrt(jnp.mean(o * o, axis=-1, keepdims=True) + RMS_EPS) * ng_ref[...]
        r = r_ref[rows, :]
        o_ref[rows, :] = (on * (r * _sigmoid(r))).astype(o_ref.dtype)


def _gla_sample_body(q_ref, k_ref, v_ref, r_ref, hg_ref, wg_ref, bg_ref, ng_ref, s0_ref, o_ref, s_ref,
                     kt_ref, at_ref, qt_ref):
    b = pl.program_id(1)
    dk = q_ref.shape[-1]

    @pl.when(b == 0)
    def _():
        la = _gla_log_decay(hg_ref[...], wg_ref[...], bg_ref[...])
        a = jnp.exp(la)
        kt_ref[...] = jnp.transpose(k_ref[...])
        at_ref[...] = jnp.transpose(a)
        qt_ref[...] = jnp.transpose(q_ref[...] * (dk ** -0.5) * a)

    ms = kt_ref.shape[1]
    sel = lax.broadcasted_iota(jnp.int32, (1, ms), 1) == b
    col = lambda ref: jnp.sum(jnp.where(sel, ref[...], 0.0), axis=1, keepdims=True)
    k_col, a_col, qe_col = col(kt_ref), col(at_ref), col(qt_ref)
    v_row = v_ref[pl.ds(b, 1), :]
    s0 = s0_ref[...]
    att = jnp.sum(qe_col * (k_col / a_col), axis=0, keepdims=True)
    o = jnp.sum(qe_col * s0, axis=0, keepdims=True) + att * v_row
    s_ref[...] = a_col * s0 + k_col * v_row
    on = o * lax.rsqrt(jnp.mean(o * o, axis=-1, keepdims=True) + RMS_EPS) * ng_ref[...]
    r = r_ref[pl.ds(b, 1), :]
    o_ref[...] = (on * (r * _sigmoid(r))).astype(o_ref.dtype)


def gla_mixer(xb, n_prompt_seqs, seq, state_in, w_in, w_gate_up, b_gate, norm_g, w_o):
    m = xb.shape[0]
    mp = n_prompt_seqs * seq
    ms = m - mp
    heads = GLA_HEADS
    qk = w_gate_up.shape[1]
    vd = w_o.shape[0]
    dk, dv = qk // heads, vd // heads
    rank = w_gate_up.shape[0]
    h = mm(xb, w_in)
    hg = h[:, 2 * qk + 2 * vd:]
    bg = b_gate.reshape(1, qk)
    ng = norm_g.reshape(1, dv)

    ts = _tile(seq, 256, GLA_CHUNK)
    ns = seq // ts
    kb, vb, rb = qk // dk, 2 * qk // dv, (2 * qk + vd) // dv
    rows = lambda b, hd, n: b * ns + n
    o_p, s_p = pl.pallas_call(
        functools.partial(_gla_prompt_body, n_chunks=ts // GLA_CHUNK),
        grid=(n_prompt_seqs, heads, ns),
        in_specs=[pl.BlockSpec((ts, dk), lambda b, hd, n: (rows(b, hd, n), hd)),
                  pl.BlockSpec((ts, dk), lambda b, hd, n: (rows(b, hd, n), kb + hd)),
                  pl.BlockSpec((ts, dv), lambda b, hd, n: (rows(b, hd, n), vb + hd)),
                  pl.BlockSpec((ts, dv), lambda b, hd, n: (rows(b, hd, n), rb + hd)),
                  pl.BlockSpec((ts, rank), lambda b, hd, n: (rows(b, hd, n), 0)),
                  pl.BlockSpec((rank, dk), lambda b, hd, n: (0, hd)),
                  pl.BlockSpec((1, dk), lambda b, hd, n: (0, hd)),
                  pl.BlockSpec((1, dv), lambda b, hd, n: (0, 0))],
        out_specs=[pl.BlockSpec((ts, dv), lambda b, hd, n: (rows(b, hd, n), hd)),
                   pl.BlockSpec((None, None, dk, dv), lambda b, hd, n: (b, hd, 0, 0))],
        out_shape=[jax.ShapeDtypeStruct((mp, vd), BF16), jax.ShapeDtypeStruct((n_prompt_seqs, heads, dk, dv), F32)],
        compiler_params=_cp("parallel", "parallel", "arbitrary"), name="gla_prompt")(h, h, h, h, hg, w_gate_up, bg, ng)

    hs = h[mp:]
    o_s, s_s = pl.pallas_call(
        _gla_sample_body, grid=(heads, ms),
        in_specs=[pl.BlockSpec((ms, dk), lambda hd, b: (0, hd)),
                  pl.BlockSpec((ms, dk), lambda hd, b: (0, kb + hd)),
                  pl.BlockSpec((ms, dv), lambda hd, b: (0, vb + hd)),
                  pl.BlockSpec((ms, dv), lambda hd, b: (0, rb + hd)),
                  pl.BlockSpec((ms, rank), lambda hd, b: (0, 0)),
                  pl.BlockSpec((rank, dk), lambda hd, b: (0, hd)),
                  pl.BlockSpec((1, dk), lambda hd, b: (0, hd)),
                  pl.BlockSpec((1, dv), lambda hd, b: (0, 0)),
                  pl.BlockSpec((None, None, dk, dv), lambda hd, b: (b, hd, 0, 0))],
        out_specs=[pl.BlockSpec((None, 1, dv), lambda hd, b: (b, 0, hd)),
                   pl.BlockSpec((None, None, dk, dv), lambda hd, b: (b, hd, 0, 0))],
        out_shape=[jax.ShapeDtypeStruct((ms, 1, vd), BF16), jax.ShapeDtypeStruct((ms, heads, dk, dv), F32)],
        scratch_shapes=[pltpu.VMEM((dk, ms), F32)] * 3,
        compiler_params=_cp("arbitrary", "arbitrary"), name="gla_sample")(hs, hs, hs, hs, hg[mp:], w_gate_up, bg, ng, state_in)

    y = mm(jnp.concatenate([o_p, o_s.reshape(ms, vd)], axis=0), w_o)
    return y, s_p, s_s


def _ln_swish(y, g, b):
    mu = jnp.mean(y, axis=-1, keepdims=True)
    yc = y - mu
    var = jnp.mean(yc * yc, axis=-1, keepdims=True)
    yn = yc * lax.rsqrt(var + LN_EPS) * g + b
    return yn * _sigmoid(yn)


def _conv_prompt_body(u_ref, w_ref, bdw_ref, g_ref, b_ref, o_ref, ext_ref, y_ref, *, tq, pad, rb):
    t = pl.program_id(1)
    width = w_ref.shape[0]
    ch = u_ref.shape[1]
    lead = pad - (width - 1)

    @pl.when(t == 0)
    def _():
        ext_ref[0:pad, :] = jnp.zeros((pad, ch), F32)

    ext_ref[pad:pad + tq, :] = u_ref[...]

    def row_block(i, carry):
        r0 = pl.multiple_of(i * rb, rb)
        for c0 in range(0, ch, LANE):
            win = ext_ref[pl.ds(r0, rb + pad), c0:c0 + LANE]
            acc = jnp.zeros((rb, LANE), F32)
            for k in range(width):
                acc = acc + w_ref[k:k + 1, c0:c0 + LANE] * win[lead + k:lead + k + rb, :]
            y_ref[pl.ds(r0, rb), c0:c0 + LANE] = acc
        return carry

    lax.fori_loop(0, tq // rb, row_block, 0)
    ext_ref[0:pad, :] = ext_ref[tq:tq + pad, :]
    o_ref[...] = _ln_swish(y_ref[...] + bdw_ref[...], g_ref[...], b_ref[...]).astype(o_ref.dtype)


def _conv_sample_body(buf_ref, u_ref, w_ref, bdw_ref, g_ref, b_ref, o_ref, s_ref):
    width = w_ref.shape[0]
    buf = buf_ref[...]
    u = u_ref[...]
    y = jnp.sum(buf * w_ref[0:width - 1, :][None], axis=1) + u * w_ref[width - 1:width, :] + bdw_ref[...]
    o_ref[...] = _ln_swish(y, g_ref[...], b_ref[...]).astype(o_ref.dtype)
    s_ref[:, 0:width - 2, :] = buf[:, 1:width - 1, :]
    s_ref[:, width - 2:width - 1, :] = u[:, None, :]


def conv_mixer(xb, n_prompt_seqs, seq, state_in, w_pw1, b_pw1, w_dw, b_dw, ln_g, ln_b, w_pw2, b_pw2):
    m, d = xb.shape
    mp = n_prompt_seqs * seq
    ms = m - mp
    ch = w_dw.shape[1]
    width = w_dw.shape[0]
    u = glu(xb, w_pw1, ch, mode="glu", bias=b_pw1, out_dtype=F32)
    vec = lambda a: a.reshape(1, ch)

    tq = _tile(seq, 256, 64)
    nt = seq // tq
    pad, rb = 32, 64
    fixed = lambda shape: pl.BlockSpec(shape, lambda b, t: (0, 0))
    y_p = pl.pallas_call(
        functools.partial(_conv_prompt_body, tq=tq, pad=pad, rb=rb), grid=(n_prompt_seqs, nt),
        in_specs=[pl.BlockSpec((tq, ch), lambda b, t: (b * nt + t, 0)), fixed((width, ch)),
                  fixed((1, ch)), fixed((1, ch)), fixed((1, ch))],
        out_specs=pl.BlockSpec((tq, ch), lambda b, t: (b * nt + t, 0)),
        out_shape=jax.ShapeDtypeStruct((mp, ch), BF16),
        scratch_shapes=[pltpu.VMEM((tq + pad, ch), F32), pltpu.VMEM((tq, ch), F32)],
        compiler_params=_cp("parallel", "arbitrary"), name="conv_prompt")(u, w_dw, vec(b_dw), vec(ln_g), vec(ln_b))
    buf_p = u[:mp].reshape(n_prompt_seqs, seq, ch)[:, seq - (width - 1):]

    gs = _tile(ms, 16, 16)
    fixed1 = lambda shape: pl.BlockSpec(shape, lambda i: (0, 0))
    y_s, buf_s = pl.pallas_call(
        _conv_sample_body, grid=(ms // gs,),
        in_specs=[pl.BlockSpec((gs, width - 1, ch), lambda i: (i, 0, 0)), pl.BlockSpec((gs, ch), lambda i: (i, 0)),
                  fixed1((width, ch)), fixed1((1, ch)), fixed1((1, ch)), fixed1((1, ch))],
        out_specs=[pl.BlockSpec((gs, ch), lambda i: (i, 0)), pl.BlockSpec((gs, width - 1, ch), lambda i: (i, 0, 0))],
        out_shape=[jax.ShapeDtypeStruct((ms, ch), BF16), jax.ShapeDtypeStruct((ms, width - 1, ch), F32)],
        compiler_params=_cp("parallel"), name="conv_sample")(state_in, u[mp:], w_dw, vec(b_dw), vec(ln_g), vec(ln_b))

    y = mm(jnp.concatenate([y_p, y_s], axis=0), w_pw2)
    return y, b_pw2, buf_p, buf_s


def glu(xb, w, half, mode, bias=None, out_dtype=BF16, tn_target=512):
    m, k = xb.shape
    tm = _tile(m, 1040, 16)
    tn = _tile(half, tn_target, LANE)
    nb = half // tn
    in_specs = [pl.BlockSpec((tm, k), lambda i, j: (i, 0)), pl.BlockSpec((k, tn), lambda i, j: (0, j)),
                pl.BlockSpec((k, tn), lambda i, j: (0, nb + j))]
    args = [xb, w, w]
    if bias is not None:
        b2 = bias.reshape(1, 2 * half)
        in_specs += [pl.BlockSpec((1, tn), lambda i, j: (0, j)), pl.BlockSpec((1, tn), lambda i, j: (0, nb + j))]
        args += [b2, b2]
    return pl.pallas_call(
        functools.partial(_glu_body, mode=mode, has_bias=bias is not None),
        grid=(m // tm, nb), in_specs=in_specs, out_specs=pl.BlockSpec((tm, tn), lambda i, j: (i, j)),
        out_shape=jax.ShapeDtypeStruct((m, half), out_dtype),
        compiler_params=_cp("parallel", "parallel"), name="glu")(*args)


def dense_ffn(xb, w_gu, w_down):
    m, d = xb.shape
    ff = w_down.shape[0]
    ffp = -(-ff // 512) * 512
    wg = jnp.pad(w_gu[:, :ff], ((0, 0), (0, ffp - ff)))
    wu = jnp.pad(w_gu[:, ff:], ((0, 0), (0, ffp - ff)))
    h = glu(xb, jnp.concatenate([wg, wu], axis=1).astype(BF16), ffp, mode="swiglu")
    wd = jnp.pad(w_down, ((0, ffp - ff), (0, 0)))
    tm, tn, tk = _tile(m, 1040, 16), 1024, 512
    return mm_acc(h, wd, lambda i, j, k: (i, k), lambda i, j, k: (k, j), (tm, tk), (tk, tn), ffp // tk, m, d, tm, tn)


def _router_body(x_ref, w_ref, sel_ref, wt_ref):
    logits = jnp.dot(x_ref[...], w_ref[...], preferred_element_type=F32, precision=lax.Precision.HIGHEST)
    n_e = logits.shape[1]
    idx = lax.broadcasted_iota(jnp.int32, logits.shape, 1)
    m1 = jnp.max(logits, axis=1, keepdims=True)
    i1 = jnp.min(jnp.where(logits == m1, idx, n_e), axis=1, keepdims=True)
    rest = jnp.where(idx == i1, -jnp.inf, logits)
    m2 = jnp.max(rest, axis=1, keepdims=True)
    i2 = jnp.min(jnp.where(rest == m2, idx, n_e), axis=1, keepdims=True)
    e2 = jnp.exp(m2 - m1)
    w1 = 1.0 / (1.0 + e2)
    w2 = e2 / (1.0 + e2)
    first = lax.broadcasted_iota(jnp.int32, sel_ref.shape, 1) == 0
    sel_ref[...] = jnp.where(first, i1, i2)
    wt_ref[...] = jnp.where(first, w1, w2)


GATHER_WINDOW = 256


def _gather_rows_body(idx_ref, x_ref, o_ref, sem, *, n_rows):
    window = min(GATHER_WINDOW, n_rows)

    def issue(p, c):
        pltpu.make_async_copy(x_ref.at[idx_ref[p]], o_ref.at[p], sem).start()
        return c

    def drain(p, c):
        pltpu.make_async_copy(x_ref.at[0], o_ref.at[p], sem).wait()
        return c

    def both(p, c):
        issue(p, c)
        return drain(p - window, c)

    lax.fori_loop(0, window, issue, 0)
    lax.fori_loop(window, n_rows, both, 0)
    lax.fori_loop(n_rows - window, n_rows, drain, 0)


def gather_rows(x, idx):
    r, d = x.shape
    n = idx.shape[0]
    x3 = x.reshape(r, d // LANE, LANE)
    out = pl.pallas_call(
        functools.partial(_gather_rows_body, n_rows=n),
        grid_spec=pltpu.PrefetchScalarGridSpec(
            num_scalar_prefetch=1, grid=(1,),
            in_specs=[pl.BlockSpec(memory_space=pl.ANY)], out_specs=pl.BlockSpec(memory_space=pl.ANY),
            scratch_shapes=[pltpu.SemaphoreType.DMA(())]),
        out_shape=jax.ShapeDtypeStruct((n, d // LANE, LANE), x.dtype),
        compiler_params=_cp("arbitrary"), name="gather_rows")(idx, x3)
    return out.reshape(n, d)


def _new_weight_block(te_ref, i):
    return jnp.logical_or(i == 0, te_ref[i] != te_ref[jnp.maximum(i - 1, 0)])


def _moe_glu_body(te_ref, tv_ref, x_ref, wa_ref, wb_ref, g_ref, o_ref, wa16_ref, wb16_ref):
    i = pl.program_id(1)

    @pl.when(_new_weight_block(te_ref, i))
    def _():
        wa16_ref[...] = wa_ref[...].astype(BF16)
        wb16_ref[...] = wb_ref[...].astype(BF16)

    @pl.when(tv_ref[i] == 1)
    def _():
        x = x_ref[...]
        a = _dot(x, wa16_ref[...])
        b = _dot(x, wb16_ref[...])
        o_ref[...] = (a * _sigmoid(a) * b * g_ref[...]).astype(o_ref.dtype)

    @pl.when(tv_ref[i] == 0)
    def _():
        o_ref[...] = jnp.zeros_like(o_ref)


def _moe_down_body(te_ref, tv_ref, h_ref, w_ref, o_ref, w16_ref):
    i = pl.program_id(1)

    @pl.when(_new_weight_block(te_ref, i))
    def _():
        w16_ref[...] = w_ref[...].astype(BF16)

    @pl.when(tv_ref[i] == 1)
    def _():
        o_ref[...] = _dot(h_ref[...], w16_ref[...])

    @pl.when(tv_ref[i] == 0)
    def _():
        o_ref[...] = jnp.zeros_like(o_ref)


MOE_TILE = 512
MOE_DOWN_TILE = 256


def moe_ffn(x32, w_router, w_gu, w_down, layer):
    m, d = x32.shape
    n_e = w_router.shape[-1]
    ffe = w_down.shape[2]
    tmr = _tile(m, 416, 8)
    sel, wts = pl.pallas_call(
        _router_body, grid=(m // tmr,),
        in_specs=[pl.BlockSpec((tmr, d), lambda i: (i, 0)), pl.BlockSpec((None, d, n_e), lambda i: (layer, 0, 0))],
        out_specs=[pl.BlockSpec((tmr, TOP_K), lambda i: (i, 0))] * 2,
        out_shape=[jax.ShapeDtypeStruct((m, TOP_K), jnp.int32), jax.ShapeDtypeStruct((m, TOP_K), F32)],
        compiler_params=_cp("parallel"), name="router")(x32, w_router)

    tm = MOE_TILE
    n_asg = TOP_K * m
    n_tiles = -(-n_asg // tm) + n_e
    n_slots = n_tiles * tm
    eid = sel.reshape(n_asg)
    onehot = (eid[:, None] == jnp.arange(n_e, dtype=jnp.int32)[None, :]).astype(jnp.int32)
    csum = jnp.cumsum(onehot, axis=0)
    rank = jnp.sum(csum * onehot, axis=1) - 1
    counts = csum[-1]
    tiles_per = (counts + tm - 1) // tm
    tiles_end = jnp.cumsum(tiles_per)
    tiles_start = tiles_end - tiles_per
    slot = tiles_start[eid] * tm + rank
    tile_ids = jnp.arange(n_tiles, dtype=jnp.int32)
    tile_expert = jnp.minimum(jnp.sum((tile_ids[:, None] >= tiles_end[None, :]).astype(jnp.int32), axis=1), n_e - 1)
    tile_valid = (tile_ids < tiles_end[-1]).astype(jnp.int32)
    order = jnp.argsort(eid, stable=True).astype(jnp.int32)
    slot_ids = jnp.arange(n_slots, dtype=jnp.int32)
    slot_expert = tile_expert[slot_ids // tm]
    slot_rank = slot_ids - tiles_start[slot_expert] * tm
    slot_used = slot_rank < counts[slot_expert]
    slot_asg = order[jnp.clip((jnp.cumsum(counts) - counts)[slot_expert] + slot_rank, 0, n_asg - 1)]
    src_token = jnp.where(slot_used, slot_asg // TOP_K, 0)
    slot_gate = jnp.where(slot_used, wts.reshape(n_asg)[slot_asg], 0.0).reshape(n_slots, 1)

    xs = gather_rows(x32, src_token).astype(BF16)

    tn = _tile(ffe, 1024, LANE)
    nb = ffe // tn
    h = pl.pallas_call(
        _moe_glu_body,
        grid_spec=pltpu.PrefetchScalarGridSpec(
            num_scalar_prefetch=2, grid=(nb, n_tiles),
            in_specs=[pl.BlockSpec((tm, d), lambda j, i, te, tv: (i, 0)),
                      pl.BlockSpec((None, None, d, tn), lambda j, i, te, tv: (layer, te[i], 0, j)),
                      pl.BlockSpec((None, None, d, tn), lambda j, i, te, tv: (layer, te[i], 0, nb + j)),
                      pl.BlockSpec((tm, 1), lambda j, i, te, tv: (i, 0))],
            out_specs=pl.BlockSpec((tm, tn), lambda j, i, te, tv: (i, j)),
            scratch_shapes=[pltpu.VMEM((d, tn), BF16), pltpu.VMEM((d, tn), BF16)]),
        out_shape=jax.ShapeDtypeStruct((n_slots, ffe), BF16),
        compiler_params=_cp("arbitrary", "arbitrary"), name="moe_glu")(tile_expert, tile_valid, xs, w_gu, w_gu, slot_gate)

    td = MOE_DOWN_TILE
    sub = tm // td
    te_d = jnp.repeat(tile_expert, sub)
    tv_d = jnp.repeat(tile_valid, sub)
    tn2 = _tile(d, 512, LANE)
    ys = pl.pallas_call(
        _moe_down_body,
        grid_spec=pltpu.PrefetchScalarGridSpec(
            num_scalar_prefetch=2, grid=(d // tn2, n_tiles * sub),
            in_specs=[pl.BlockSpec((td, ffe), lambda j, i, te, tv: (i, 0)),
                      pl.BlockSpec((None, None, ffe, tn2), lambda j, i, te, tv: (layer, te[i], 0, j))],
            out_specs=pl.BlockSpec((td, tn2), lambda j, i, te, tv: (i, j)),
            scratch_shapes=[pltpu.VMEM((ffe, tn2), BF16)]),
        out_shape=jax.ShapeDtypeStruct((n_slots, d), F32),
        compiler_params=_cp("arbitrary", "arbitrary"), name="moe_down")(te_d, tv_d, h, w_down)

    return gather_rows(ys, slot.reshape(m, TOP_K).T.reshape(n_asg))


@jax.jit
def _step(x_prompt, x_sample, cache_mla_latent, cache_mla_rope, cache_sb_k, cache_sb_v, state_gla, state_conv,
          page_table, ln_g, ln_b, mla_w_in, mla_q_norm, mla_kv_norm, mla_w_q_up, mla_w_kv_up, mla_w_o,
          sb_w_in, sb_w_o, gla_w_in, gla_w_gate_up, gla_b_gate, gla_norm, gla_w_o,
          conv_w_pw1, conv_b_pw1, conv_w_dw, conv_b_dw, conv_ln_g, conv_ln_b, conv_w_pw2, conv_b_pw2,
          ffn_w_gu, ffn_w_down, moe_w_router, moe_w_gu, moe_w_down):
    bsz, seq, d = x_prompt.shape
    dec = x_sample.shape[0]
    assert x_sample.shape[1] == 1, "one new token per sampled sequence"
    assert ln_g.shape[0] == DEPTH
    mp = bsz * seq
    past = page_table.shape[1] * cache_mla_latent.shape[2]
    x32 = jnp.concatenate([x_prompt.reshape(mp, d), x_sample.reshape(dec, d)], axis=0)
    xb = x32.astype(BF16)
    outs = {}
    for i in range(DEPTH):
        mixer = i % 4
        bias = None
        if mixer == 0:
            y, c32, r32 = mla_mixer(xb, bsz, seq, past, cache_mla_latent, cache_mla_rope, page_table,
                                    mla_w_in[0], mla_q_norm[0], mla_kv_norm[0], mla_w_q_up[0], mla_w_kv_up[0], mla_w_o[0])
            cr, rr = c32.shape[1], r32.shape[1]
            outs["mla"] = (c32[:mp].reshape(1, bsz, seq, cr), r32[:mp].reshape(1, bsz, seq, rr),
                           c32[mp:].reshape(1, dec, 1, cr), r32[mp:].reshape(1, dec, 1, rr))
        elif mixer == 1:
            y, k_new, v_new = sb_mixer(xb, bsz, seq, cache_sb_k, cache_sb_v, page_table, sb_w_in[0], sb_w_o[0])
            kv = lambda a, lo, hi, b, t: a[lo:hi].reshape(1, b, t, SB_KV_HEADS, SB_HEAD_DIM)
            outs["sb"] = (kv(k_new, 0, mp, bsz, seq), kv(v_new, 0, mp, bsz, seq),
                          kv(k_new, mp, mp + dec, dec, 1), kv(v_new, mp, mp + dec, dec, 1))
        elif mixer == 2:
            y, s_p, s_s = gla_mixer(xb, bsz, seq, state_gla[0], gla_w_in[0], gla_w_gate_up[0], gla_b_gate[0],
                                    gla_norm[0], gla_w_o[0])
            outs["gla"] = (s_p[None], s_s[None])
        else:
            y, bias, buf_p, buf_s = conv_mixer(xb, bsz, seq, state_conv[0], conv_w_pw1[0], conv_b_pw1[0], conv_w_dw[0],
                                               conv_b_dw[0], conv_ln_g[0], conv_ln_b[0], conv_w_pw2[0], conv_b_pw2[0])
            outs["conv"] = (buf_p[None], buf_s[None])
        x32, xb = res_ln(x32, y, ln_g[i, 0], ln_b[i, 0], bias=bias)
        f = i // 2
        if i % 2 == 0:
            y = dense_ffn(xb, ffn_w_gu[f], ffn_w_down[f])
        else:
            y = moe_ffn(x32, moe_w_router, moe_w_gu, moe_w_down, f)
        x32, xb = res_ln(x32, y, ln_g[i, 1], ln_b[i, 1])
    return (x32[:mp].reshape(bsz, seq, d), x32[mp:].reshape(dec, 1, d)) + outs["mla"] + outs["sb"] + outs["gla"] + outs["conv"]


def kernel(x_prompt, x_sample, cache_mla_latent, cache_mla_rope, cache_sb_k, cache_sb_v, state_gla, state_conv, page_table, ln_g, ln_b, mla_w_in, mla_q_norm, mla_kv_norm, mla_w_q_up, mla_w_kv_up, mla_w_o, sb_w_in, sb_w_o, gla_w_in, gla_w_gate_up, gla_b_gate, gla_norm, gla_w_o, conv_w_pw1, conv_b_pw1, conv_w_dw, conv_b_dw, conv_ln_g, conv_ln_b, conv_w_pw2, conv_b_pw2, ffn_w_gu, ffn_w_down, moe_w_router, moe_w_gu, moe_w_down):
    return _step(x_prompt, x_sample, cache_mla_latent, cache_mla_rope, cache_sb_k, cache_sb_v, state_gla, state_conv,
                 page_table, ln_g, ln_b, mla_w_in, mla_q_norm, mla_kv_norm, mla_w_q_up, mla_w_kv_up, mla_w_o,
                 sb_w_in, sb_w_o, gla_w_in, gla_w_gate_up, gla_b_gate, gla_norm, gla_w_o,
                 conv_w_pw1, conv_b_pw1, conv_w_dw, conv_b_dw, conv_ln_g, conv_ln_b, conv_w_pw2, conv_b_pw2,
                 ffn_w_gu, ffn_w_down, moe_w_router, moe_w_gu, moe_w_down)
```
